```python
import math
import jax, jax.numpy as jnp
from jax import lax
import numpy as np

D_MODEL = 2048
BATCH = 2
SEQ = 4096
DEPTH = 4
DEC_BATCH = 8
DEC_SEQ = 1
PAST_LEN = 16384
PAGE_SIZE = 128

N_MIXERS = 2
N_DIFF_LAYERS = (DEPTH + 1) // 2
N_GLA_LAYERS = DEPTH // 2
ATTN_HEADS = 8
ATTN_HEAD_DIM = D_MODEL // ATTN_HEADS // 2
ATTN_V_DIM = 2 * ATTN_HEAD_DIM
D_ATTN = ATTN_HEADS * 2 * ATTN_HEAD_DIM
ROT_DIM = ATTN_HEAD_DIM // 4
ROPE_THETA = 500000.0
Q_BLOCK = 128
GLA_HEADS = 4
GLA_KEY_DIM = D_MODEL // 2
GLA_VALUE_DIM = D_MODEL
GLA_DK = GLA_KEY_DIM // GLA_HEADS
GLA_DV = GLA_VALUE_DIM // GLA_HEADS
GLA_GATE_RANK = 16
GLA_GATE_NORMALIZER = 16.0
GLA_CHUNK = 64
D_FF = 128 * ((8 * D_MODEL // 3 + 127) // 128)
FFN_RESIDUAL_SCALE = 0.5
RMS_EPS = 1e-6

kernel_name = 'hybrid_diffattn_gla_macaron_step'


def rms_norm(x, gain):
    xf = x.astype(jnp.float32)
    y = xf * lax.rsqrt(jnp.mean(xf * xf, axis=-1, keepdims=True) + RMS_EPS)
    return (y * gain.astype(jnp.float32)).astype(x.dtype)


def ffn_half(x, gain, w_gate, w_up, w_down):
    h = rms_norm(x, gain)
    return x + FFN_RESIDUAL_SCALE * ((jax.nn.silu(h @ w_gate) * (h @ w_up)) @ w_down)


def partial_rotary(x, pos):
    half = ROT_DIM // 2
    inv_freq = jnp.power(ROPE_THETA, -jnp.arange(0, ROT_DIM, 2, dtype=jnp.float32) / ROT_DIM)
    ang = pos.astype(jnp.float32)[:, None] * inv_freq[None, :]
    cos = jnp.cos(ang)[:, None, None, :]
    sin = jnp.sin(ang)[:, None, None, :]
    xr = x[..., :ROT_DIM].astype(jnp.float32)
    x1, x2 = xr[..., :half], xr[..., half:]
    rot = jnp.concatenate([x1 * cos - x2 * sin, x2 * cos + x1 * sin], axis=-1)
    return jnp.concatenate([rot.astype(x.dtype), x[..., ROT_DIM:]], axis=-1)


def diff_attn_project(h, w_qkv, pos):
    B, T, _ = h.shape
    q, k, v = jnp.split(h @ w_qkv, 3, axis=-1)
    q = partial_rotary(q.reshape(B, T, ATTN_HEADS, 2, ATTN_HEAD_DIM), pos)
    k = partial_rotary(k.reshape(B, T, ATTN_HEADS, 2, ATTN_HEAD_DIM), pos)
    v = v.reshape(B, T, ATTN_HEADS, ATTN_V_DIM)
    return q, k, v


def diff_lambda(lq1, lk1, lq2, lk2, lam_init):
    f = jnp.float32
    return (jnp.exp(jnp.sum(lq1.astype(f) * lk1.astype(f)))
            - jnp.exp(jnp.sum(lq2.astype(f) * lk2.astype(f))) + lam_init)


def diff_attn_core(q, k, v, q_pos, k_pos, lam):
    s = jnp.einsum('bqhcd,bkhcd->bhcqk', q, k).astype(jnp.float32) * (ATTN_HEAD_DIM ** -0.5)
    s = jnp.where(k_pos[None, :] <= q_pos[:, None], s, -jnp.inf)
    p = jax.nn.softmax(s, axis=-1)
    a = (p[:, :, 0] - lam * p[:, :, 1]).astype(v.dtype)
    return jnp.einsum('bhqk,bkhe->bqhe', a, v)


def diff_attn_output(o, subln, lam_init, w_o):
    B, T = o.shape[:2]
    o = rms_norm(o, subln) * (1.0 - lam_init)
    return o.reshape(B, T, D_ATTN) @ w_o


def diff_attn_prompt(h, w_qkv, lam, lam_init, subln, w_o):
    B, T, _ = h.shape
    pos = jnp.arange(T)
    q, k, v = diff_attn_project(h, w_qkv, pos)
    nb = T // Q_BLOCK
    qb = q.reshape(B, nb, Q_BLOCK, ATTN_HEADS, 2, ATTN_HEAD_DIM).swapaxes(0, 1)
    pb = pos.reshape(nb, Q_BLOCK)
    o = lax.map(lambda a: diff_attn_core(a[0], k, v, a[1], pos, lam), (qb, pb))
    o = o.swapaxes(0, 1).reshape(B, T, ATTN_HEADS, ATTN_V_DIM)
    y = diff_attn_output(o, subln, lam_init, w_o)
    return y, k.reshape(B, T, ATTN_HEADS, 2 * ATTN_HEAD_DIM), v


def diff_attn_sample(h, cache_k, cache_v, layer, page_table, w_qkv, lam, lam_init, subln, w_o):
    Bs, T, _ = h.shape
    n_past = page_table.shape[1] * PAGE_SIZE
    pos = n_past + jnp.arange(T)
    q, k, v = diff_attn_project(h, w_qkv, pos)
    past_k = cache_k[layer, page_table].reshape(Bs, n_past, ATTN_HEADS, 2, ATTN_HEAD_DIM)
    past_v = cache_v[layer, page_table].reshape(Bs, n_past, ATTN_HEADS, ATTN_V_DIM)
    k_all = jnp.concatenate([past_k.astype(k.dtype), k], axis=1)
    v_all = jnp.concatenate([past_v.astype(v.dtype), v], axis=1)
    o = diff_attn_core(q, k_all, v_all, pos, jnp.arange(n_past + T), lam)
    y = diff_attn_output(o, subln, lam_init, w_o)
    return y, k.reshape(Bs, T, ATTN_HEADS, 2 * ATTN_HEAD_DIM), v


def gla_project(h, w_proj, w_gk1, w_gk2, b_gk):
    B, T, _ = h.shape
    q, k, v, og = jnp.split(h @ w_proj, [GLA_KEY_DIM, 2 * GLA_KEY_DIM, 2 * GLA_KEY_DIM + GLA_VALUE_DIM], axis=-1)
    gk = (h @ w_gk1) @ w_gk2 + b_gk
    log_a = jax.nn.log_sigmoid(gk.astype(jnp.float32)) / GLA_GATE_NORMALIZER
    f32 = lambda a, d: a.astype(jnp.float32).reshape(B, T, GLA_HEADS, d)
    return (f32(q, GLA_DK) * (GLA_DK ** -0.5), f32(k, GLA_DK), f32(v, GLA_DV),
            log_a.reshape(B, T, GLA_HEADS, GLA_DK), og)


def gla_output(o, og, norm_g, w_o):
    B, T = o.shape[:2]
    o = rms_norm(o, norm_g).reshape(B, T, GLA_VALUE_DIM).astype(og.dtype)
    return (o * jax.nn.silu(og)) @ w_o


def gla_chunked(q, k, v, log_a):
    B, T, H, DK = q.shape
    C = GLA_CHUNK
    n = T // C
    ch = lambda a: a.reshape(B, n, C, H, a.shape[-1]).transpose(1, 0, 3, 2, 4)
    qc, kc, vc, gc = ch(q), ch(k), ch(v), ch(log_a)
    b = jnp.cumsum(gc, axis=3)
    b_last = b[:, :, :, -1, :]
    q_e = qc * jnp.exp(b)
    k_e = kc * jnp.exp(-b)
    k_tail = kc * jnp.exp(b_last[:, :, :, None, :] - b)
    causal = jnp.tril(jnp.ones((C, C), dtype=bool))
    attn = jnp.where(causal, jnp.einsum('nbhtk,nbhsk->nbhts', q_e, k_e), 0.0)
    o_intra = jnp.einsum('nbhts,nbhsv->nbhtv', attn, vc)

    def step(S, inp):
        q_c, kt_c, v_c, bl_c = inp
        o_inter = jnp.einsum('bhtk,bhkv->bhtv', q_c, S)
        S = jnp.exp(bl_c)[..., None] * S + jnp.einsum('bhsk,bhsv->bhkv', kt_c, v_c)
        return S, o_inter

    S0 = jnp.zeros((B, H, DK, v.shape[-1]), jnp.float32)
    S, o_inter = lax.scan(step, S0, (q_e, k_tail, vc, b_last))
    o = (o_intra + o_inter).transpose(1, 0, 3, 2, 4).reshape(B, T, H, v.shape[-1])
    return o, S


def gla_recurrent(q, k, v, log_a, S0):
    def step(S, inp):
        q_t, k_t, v_t, g_t = inp
        S = jnp.exp(g_t)[..., None] * S + k_t[..., :, None] * v_t[..., None, :]
        return S, jnp.einsum('bhk,bhkv->bhv', q_t, S)

    xs = (q.swapaxes(0, 1), k.swapaxes(0, 1), v.swapaxes(0, 1), log_a.swapaxes(0, 1))
    S, o = lax.scan(step, S0.astype(jnp.float32), xs)
    return o.swapaxes(0, 1), S


def gla_prompt(h, w_proj, w_gk1, w_gk2, b_gk, norm_g, w_o):
    q, k, v, g, og = gla_project(h, w_proj, w_gk1, w_gk2, b_gk)
    o, S = gla_chunked(q, k, v, g)
    return gla_output(o, og, norm_g, w_o), S


def gla_sample(h, S0, w_proj, w_gk1, w_gk2, b_gk, norm_g, w_o):
    q, k, v, g, og = gla_project(h, w_proj, w_gk1, w_gk2, b_gk)
    o, S = gla_recurrent(q, k, v, g, S0)
    return gla_output(o, og, norm_g, w_o), S


def setup_inputs(seed: int = 0) -> dict:
    key = jax.random.key(seed)
    ks = jax.random.split(key, 24)
    nrm = lambda k, shape, scale: jax.random.normal(k, shape, jnp.float32) * scale
    n_pages = PAST_LEN // PAGE_SIZE
    n_used = DEC_BATCH * n_pages
    n_phys = n_used + max(1, n_used // 4)
    page_table = jax.random.permutation(ks[0], n_phys)[:n_used].reshape(DEC_BATCH, n_pages).astype(jnp.int32)
    return {
        'x_prompt': nrm(ks[1], (BATCH, SEQ, D_MODEL), 1.0),
        'x_sample': nrm(ks[2], (DEC_BATCH, DEC_SEQ, D_MODEL), 1.0),
        'cache_k': nrm(ks[3], (N_DIFF_LAYERS, n_phys, PAGE_SIZE, ATTN_HEADS, 2 * ATTN_HEAD_DIM), 1.0),
        'cache_v': nrm(ks[4], (N_DIFF_LAYERS, n_phys, PAGE_SIZE, ATTN_HEADS, ATTN_V_DIM), 1.0),
        'state_gla': nrm(ks[5], (N_GLA_LAYERS, DEC_BATCH, GLA_HEADS, GLA_DK, GLA_DV), 1.0),
        'page_table': page_table,
        'norm_sub': 1.0 + nrm(ks[6], (DEPTH, 3, D_MODEL), 0.02),
        'final_norm': 1.0 + nrm(ks[7], (D_MODEL,), 0.02),
        'w_ffn_gate': nrm(ks[8], (DEPTH, 2, D_MODEL, D_FF), D_MODEL ** -0.5),
        'w_ffn_up': nrm(ks[9], (DEPTH, 2, D_MODEL, D_FF), D_MODEL ** -0.5),
        'w_ffn_down': nrm(ks[10], (DEPTH, 2, D_FF, D_MODEL), D_FF ** -0.5),
        'w_qkv_a': nrm(ks[11], (N_DIFF_LAYERS, D_MODEL, 3 * D_ATTN), D_MODEL ** -0.5),
        'lambda_q1': nrm(ks[12], (N_DIFF_LAYERS, ATTN_HEAD_DIM), 0.1),
        'lambda_k1': nrm(ks[13], (N_DIFF_LAYERS, ATTN_HEAD_DIM), 0.1),
        'lambda_q2': nrm(ks[14], (N_DIFF_LAYERS, ATTN_HEAD_DIM), 0.1),
        'lambda_k2': nrm(ks[15], (N_DIFF_LAYERS, ATTN_HEAD_DIM), 0.1),
        'subln_a': 1.0 + nrm(ks[16], (N_DIFF_LAYERS, ATTN_V_DIM), 0.02),
        'w_o_a': nrm(ks[17], (N_DIFF_LAYERS, D_ATTN, D_MODEL), D_ATTN ** -0.5),
        'w_proj_g': nrm(ks[18], (N_GLA_LAYERS, D_MODEL, 2 * GLA_KEY_DIM + 2 * GLA_VALUE_DIM), D_MODEL ** -0.5),
        'w_gk1': nrm(ks[19], (N_GLA_LAYERS, D_MODEL, GLA_GATE_RANK), D_MODEL ** -0.5),
        'w_gk2': nrm(ks[20], (N_GLA_LAYERS, GLA_GATE_RANK, GLA_KEY_DIM), GLA_GATE_RANK ** -0.5),
        'b_gk': nrm(ks[21], (N_GLA_LAYERS, GLA_KEY_DIM), 0.1),
        'norm_g': 1.0 + nrm(ks[22], (N_GLA_LAYERS, GLA_DV), 0.02),
        'w_o_g': nrm(ks[23], (N_GLA_LAYERS, GLA_VALUE_DIM, D_MODEL), GLA_VALUE_DIM ** -0.5),
    }


def reference(x_prompt, x_sample, cache_k, cache_v, state_gla, page_table, norm_sub, final_norm,
              w_ffn_gate, w_ffn_up, w_ffn_down, w_qkv_a, lambda_q1, lambda_k1, lambda_q2, lambda_k2,
              subln_a, w_o_a, w_proj_g, w_gk1, w_gk2, b_gk, norm_g, w_o_g):
    xp, xs = x_prompt, x_sample
    k_p, v_p, k_s, v_s, st_p, st_s = [], [], [], [], [], []
    for i in range(DEPTH):
        j = i // N_MIXERS
        xp = ffn_half(xp, norm_sub[i, 0], w_ffn_gate[i, 0], w_ffn_up[i, 0], w_ffn_down[i, 0])
        xs = ffn_half(xs, norm_sub[i, 0], w_ffn_gate[i, 0], w_ffn_up[i, 0], w_ffn_down[i, 0])
        hp = rms_norm(xp, norm_sub[i, 1])
        hs = rms_norm(xs, norm_sub[i, 1])
        if i % N_MIXERS == 0:
            lam_init = 0.8 - 0.6 * math.exp(-0.3 * i)
            lam = diff_lambda(lambda_q1[j], lambda_k1[j], lambda_q2[j], lambda_k2[j], lam_init)
            yp, kr, vr = diff_attn_prompt(hp, w_qkv_a[j], lam, lam_init, subln_a[j], w_o_a[j])
            k_p.append(kr)
            v_p.append(vr)
            ys, kr, vr = diff_attn_sample(hs, cache_k, cache_v, j, page_table, w_qkv_a[j], lam, lam_init, subln_a[j], w_o_a[j])
            k_s.append(kr)
            v_s.append(vr)
        else:
            yp, sp = gla_prompt(hp, w_proj_g[j], w_gk1[j], w_gk2[j], b_gk[j], norm_g[j], w_o_g[j])
            ys, ss = gla_sample(hs, state_gla[j], w_proj_g[j], w_gk1[j], w_gk2[j], b_gk[j], norm_g[j], w_o_g[j])
            st_p.append(sp.astype(xp.dtype))
            st_s.append(ss.astype(state_gla.dtype))
        xp = xp + yp
        xs = xs + ys
        xp = ffn_half(xp, norm_sub[i, 2], w_ffn_gate[i, 1], w_ffn_up[i, 1], w_ffn_down[i, 1])
        xs = ffn_half(xs, norm_sub[i, 2], w_ffn_gate[i, 1], w_ffn_up[i, 1], w_ffn_down[i, 1])
    return (rms_norm(xp, final_norm), rms_norm(xs, final_norm), jnp.stack(k_p), jnp.stack(v_p), jnp.stack(st_p), jnp.stack(k_s), jnp.stack(v_s), jnp.stack(st_s))
```

```python
import functools
import math

import jax
import jax.numpy as jnp
from jax import lax
from jax.experimental import pallas as pl
from jax.experimental.pallas import tpu as pltpu

F32 = jnp.float32
BF16 = jnp.bfloat16

RMS_EPS = 1e-6
ROPE_THETA = 500000.0
FFN_RESIDUAL_SCALE = 0.5
LANES = 128
BF16_SUBLANES = 16
VMEM_LIMIT_BYTES = 56 * 1024 * 1024

CFG = dict(
    attn_heads=8, gla_heads=4, rot_dim=32, gla_gate_norm=16.0, gla_chunk=64, n_mixers=2,
    ffn_tm=512, ffn_tf=512, proj_tm=512, proj_tn=512, attn_t=512, gla_rows=256,
)


def _cparams(*sem):
    return pltpu.CompilerParams(dimension_semantics=sem, vmem_limit_bytes=VMEM_LIMIT_BYTES)


def _rms(x, gain):
    return x * lax.rsqrt(jnp.mean(x * x, axis=-1, keepdims=True) + RMS_EPS) * gain


def _silu(x):
    return x * jax.nn.sigmoid(x)


def _dot(a, b):
    return jnp.dot(a, b, preferred_element_type=F32)


def _dot_nt(a, b):
    return lax.dot_general(a, b, (((1,), (1,)), ((), ())), preferred_element_type=F32)


def _dot_tn(a, b):
    return lax.dot_general(a, b, (((0,), (0,)), ((), ())), preferred_element_type=F32)


def _ffn_kernel(x_ref, g_ref, wg_ref, wu_ref, wd_ref, o_ref, h_ref):
    f = pl.program_id(1)

    @pl.when(f == 0)
    def _():
        h_ref[...] = _rms(x_ref[...], g_ref[...]).astype(BF16)
        o_ref[...] = jnp.zeros_like(o_ref)

    h = h_ref[...]
    a = (_silu(_dot(h, wg_ref[...])) * _dot(h, wu_ref[...])).astype(BF16)
    o_ref[...] += _dot(a, wd_ref[...])

    @pl.when(f == pl.num_programs(1) - 1)
    def _():
        o_ref[...] = x_ref[...] + FFN_RESIDUAL_SCALE * o_ref[...]


def ffn_half(x, gain, wg, wu, wd, *, tm, tf):
    m, d = x.shape
    ffp = wg.shape[1]
    tm = min(tm, m)
    return pl.pallas_call(
        _ffn_kernel,
        grid=(m // tm, ffp // tf),
        in_specs=[
            pl.BlockSpec((tm, d), lambda i, f: (i, 0)),
            pl.BlockSpec((1, d), lambda i, f: (0, 0)),
            pl.BlockSpec((d, tf), lambda i, f: (0, f)),
            pl.BlockSpec((d, tf), lambda i, f: (0, f)),
            pl.BlockSpec((tf, d), lambda i, f: (f, 0)),
        ],
        out_specs=pl.BlockSpec((tm, d), lambda i, f: (i, 0)),
        out_shape=jax.ShapeDtypeStruct((m, d), F32),
        scratch_shapes=[pltpu.VMEM((tm, d), BF16)],
        compiler_params=_cparams("parallel", "arbitrary"),
        name="ffn_half",
    )(x, gain.reshape(1, d), wg, wu, wd)


def _norm_matmul_kernel(x_ref, g_ref, w_ref, o_ref, h_ref):
    @pl.when(pl.program_id(1) == 0)
    def _():
        h_ref[...] = _rms(x_ref[...], g_ref[...]).astype(BF16)

    o_ref[...] = _dot(h_ref[...], w_ref[...])


def norm_matmul(x, gain, w, *, tm, tn):
    m, d = x.shape
    n = w.shape[1]
    tm = min(tm, m)
    return pl.pallas_call(
        _norm_matmul_kernel,
        grid=(m // tm, n // tn),
        in_specs=[
            pl.BlockSpec((tm, d), lambda i, j: (i, 0)),
            pl.BlockSpec((1, d), lambda i, j: (0, 0)),
            pl.BlockSpec((d, tn), lambda i, j: (0, j)),
        ],
        out_specs=pl.BlockSpec((tm, tn), lambda i, j: (i, j)),
        out_shape=jax.ShapeDtypeStruct((m, n), F32),
        scratch_shapes=[pltpu.VMEM((tm, d), BF16)],
        compiler_params=_cparams("parallel", "arbitrary"),
        name="norm_matmul",
    )(x, gain.reshape(1, d), w)


def _matmul_residual_kernel(a_ref, w_ref, x_ref, o_ref):
    o_ref[...] = x_ref[...] + _dot(a_ref[...], w_ref[...])


def matmul_residual(a, w, x, *, tm, tn):
    m, k = a.shape
    n = w.shape[1]
    tm = min(tm, m)
    return pl.pallas_call(
        _matmul_residual_kernel,
        grid=(m // tm, n // tn),
        in_specs=[
            pl.BlockSpec((tm, k), lambda i, j: (i, 0)),
            pl.BlockSpec((k, tn), lambda i, j: (0, j)),
            pl.BlockSpec((tm, tn), lambda i, j: (i, j)),
        ],
        out_specs=pl.BlockSpec((tm, tn), lambda i, j: (i, j)),
        out_shape=jax.ShapeDtypeStruct((m, n), F32),
        compiler_params=_cparams("parallel", "arbitrary"),
        name="matmul_residual",
    )(a, w, x)


def _final_norm_kernel(x_ref, g_ref, o_ref):
    o_ref[...] = _rms(x_ref[...], g_ref[...])


def final_norm(x, gain, *, tm):
    m, d = x.shape
    tm = min(tm, m)
    return pl.pallas_call(
        _final_norm_kernel,
        grid=(m // tm,),
        in_specs=[pl.BlockSpec((tm, d), lambda i: (i, 0)), pl.BlockSpec((1, d), lambda i: (0, 0))],
        out_specs=pl.BlockSpec((tm, d), lambda i: (i, 0)),
        out_shape=jax.ShapeDtypeStruct((m, d), F32),
        compiler_params=_cparams("parallel"),
        name="final_norm",
    )(x, gain.reshape(1, d))


def _rotary_tables(pos, rot_dim):
    half = rot_dim // 2
    inv_freq = jnp.power(ROPE_THETA, -jnp.arange(0, rot_dim, 2, dtype=F32) / rot_dim)
    ang = pos.astype(F32)[:, None] * inv_freq[None, :]
    cos, sin = jnp.cos(ang), jnp.sin(ang)
    n = pos.shape[0]
    cos_t = jnp.concatenate([cos, cos, jnp.ones((n, LANES - rot_dim), F32)], axis=-1)
    sin_t = jnp.concatenate([-sin, sin, jnp.zeros((n, LANES - rot_dim), F32)], axis=-1)
    del half
    return cos_t, sin_t


def _rotate(t, cos_t, sin_t, rot_dim):
    half = rot_dim // 2
    lane = lax.broadcasted_iota(jnp.int32, (t.shape[0], LANES), 1)
    segs = []
    for s in range(t.shape[1] // LANES):
        xs = t[:, s * LANES:(s + 1) * LANES]
        up = pltpu.roll(xs, LANES - half, 1)
        down = pltpu.roll(xs, half, 1)
        partner = jnp.where(lane < half, up, down)
        segs.append(jnp.where(lane < rot_dim, xs * cos_t + partner * sin_t, xs))
    return jnp.concatenate(segs, axis=1)


def _qkv_kernel(x_ref, g_ref, wq_ref, wk_ref, wv_ref, cos_ref, sin_ref,
                qb_ref, kb_ref, vb_ref, kf_ref, vf_ref, h_ref, *, rot_dim):
    @pl.when(pl.program_id(1) == 0)
    def _():
        h_ref[...] = _rms(x_ref[...], g_ref[...]).astype(BF16)

    h = h_ref[...]
    cos_t, sin_t = cos_ref[...], sin_ref[...]
    q = _rotate(_dot(h, wq_ref[...]), cos_t, sin_t, rot_dim)
    k = _rotate(_dot(h, wk_ref[...]), cos_t, sin_t, rot_dim)
    v = _dot(h, wv_ref[...])
    qb_ref[...] = q.astype(BF16)
    kb_ref[...] = k.astype(BF16)
    vb_ref[...] = v.astype(BF16)
    kf_ref[...] = k
    vf_ref[...] = v


def qkv_project(x, gain, w_qkv, cos_t, sin_t, *, tm, tn, rot_dim):
    m, d = x.shape
    da = w_qkv.shape[1] // 3
    tm = min(tm, m)
    nj = da // tn
    ntab = cos_t.shape[0] // tm
    wspec = lambda sec: pl.BlockSpec((d, tn), lambda i, j: (0, sec * nj + j))
    tspec = pl.BlockSpec((tm, LANES), lambda i, j: (i % ntab, 0))
    ospec = pl.BlockSpec((tm, tn), lambda i, j: (i, j))
    return pl.pallas_call(
        functools.partial(_qkv_kernel, rot_dim=rot_dim),
        grid=(m // tm, nj),
        in_specs=[
            pl.BlockSpec((tm, d), lambda i, j: (i, 0)),
            pl.BlockSpec((1, d), lambda i, j: (0, 0)),
            wspec(0), wspec(1), wspec(2), tspec, tspec,
        ],
        out_specs=[ospec] * 5,
        out_shape=[jax.ShapeDtypeStruct((m, da), BF16)] * 3 + [jax.ShapeDtypeStruct((m, da), F32)] * 2,
        scratch_shapes=[pltpu.VMEM((tm, d), BF16)],
        compiler_params=_cparams("parallel", "arbitrary"),
        name="qkv_project",
    )(x, gain.reshape(1, d), w_qkv, w_qkv, w_qkv, cos_t, sin_t)


def _lambda(lq1_ref, lk1_ref, lq2_ref, lk2_ref, lam_init):
    e1 = jnp.exp(jnp.sum(lq1_ref[...] * lk1_ref[...], axis=-1, keepdims=True))
    e2 = jnp.exp(jnp.sum(lq2_ref[...] * lk2_ref[...], axis=-1, keepdims=True))
    return e1 - e2 + lam_init


def _attn_kernel(qi_ref, kj_ref, q_ref, k_ref, v_ref, lq1_ref, lk1_ref, lq2_ref, lk2_ref, sub_ref,
                 o_ref, m_ref, l_ref, acc_ref, *, t, hd, lam_init):
    s = pl.program_id(2)
    i = qi_ref[s]
    j = kj_ref[s]

    @pl.when(j == 0)
    def _():
        m_ref[...] = jnp.full_like(m_ref, -jnp.inf)
        l_ref[...] = jnp.zeros_like(l_ref)
        acc_ref[...] = jnp.zeros_like(acc_ref)

    q = q_ref[...]
    k = k_ref[...]
    v = v_ref[...]
    row = lax.broadcasted_iota(jnp.int32, (t, t), 0) + i * t
    col = lax.broadcasted_iota(jnp.int32, (t, t), 1) + j * t
    visible = col <= row
    for c in range(2):
        sc = _dot_nt(q[:, c * hd:(c + 1) * hd], k[:, c * hd:(c + 1) * hd]) * (hd ** -0.5)
        sc = jnp.where(visible, sc, -jnp.inf)
        m_old = m_ref[c]
        m_new = jnp.maximum(m_old, jnp.max(sc, axis=-1, keepdims=True))
        alpha = jnp.exp(m_old - m_new)
        p = jnp.exp(sc - m_new)
        l_ref[c] = alpha * l_ref[c] + jnp.sum(p, axis=-1, keepdims=True)
        acc_ref[c] = alpha * acc_ref[c] + _dot(p.astype(BF16), v)
        m_ref[c] = m_new

    @pl.when(j == i)
    def _():
        lam = _lambda(lq1_ref, lk1_ref, lq2_ref, lk2_ref, lam_init)
        o = acc_ref[0] / l_ref[0] - lam * (acc_ref[1] / l_ref[1])
        o_ref[...] = (_rms(o, sub_ref[...]) * (1.0 - lam_init)).astype(BF16)


def diff_attention_prompt(q, k, v, lam_params, subln, *, batch, heads, t, lam_init):
    m, da = q.shape
    vd = da // heads
    hd = vd // 2
    seq = m // batch
    t = min(t, seq)
    nq = seq // t
    pairs = [(i, j) for i in range(nq) for j in range(i + 1)]
    qi = jnp.asarray([p[0] for p in pairs], jnp.int32)
    kj = jnp.asarray([p[1] for p in pairs], jnp.int32)
    qspec = pl.BlockSpec((t, vd), lambda b, h, s, qi, kj: (b * nq + qi[s], h))
    kspec = pl.BlockSpec((t, vd), lambda b, h, s, qi, kj: (b * nq + kj[s], h))
    pspec = pl.BlockSpec((1, hd), lambda b, h, s, qi, kj: (0, 0))
    return pl.pallas_call(
        functools.partial(_attn_kernel, t=t, hd=hd, lam_init=lam_init),
        grid_spec=pltpu.PrefetchScalarGridSpec(
            num_scalar_prefetch=2,
            grid=(batch, heads, len(pairs)),
            in_specs=[qspec, kspec, kspec, pspec, pspec, pspec, pspec,
                      pl.BlockSpec((1, vd), lambda b, h, s, qi, kj: (0, 0))],
            out_specs=qspec,
            scratch_shapes=[pltpu.VMEM((2, t, 1), F32), pltpu.VMEM((2, t, 1), F32),
                            pltpu.VMEM((2, t, vd), F32)],
        ),
        out_shape=jax.ShapeDtypeStruct((m, da), BF16),
        compiler_params=_cparams("parallel", "parallel", "arbitrary"),
        name="diff_attention_prompt",
    )(qi, kj, q, k, v, *[p.reshape(1, hd) for p in lam_params], subln.reshape(1, vd))


def _decode_kernel(pt_ref, q_ref, kc_ref, vc_ref, kn_ref, vn_ref, lq1_ref, lk1_ref, lq2_ref, lk2_ref,
                   sub_ref, o_ref, m_ref, l_ref, acc_ref, *, heads, hd, lam_init):
    p = pl.program_id(1)
    rows = 2 * heads

    @pl.when(p == 0)
    def _():
        m_ref[...] = jnp.full_like(m_ref, -jnp.inf)
        l_ref[...] = jnp.zeros_like(l_ref)
        acc_ref[...] = jnp.zeros_like(acc_ref)

    q = q_ref[...]
    lane = lax.broadcasted_iota(jnp.int32, q.shape, 1)
    qmat = jnp.concatenate([jnp.where(lane < hd, q, 0.0), jnp.where(lane >= hd, q, 0.0)], axis=0).astype(BF16)

    def online_update(sc, pv_of):
        m_old = m_ref[...]
        m_new = jnp.maximum(m_old, jnp.max(sc, axis=-1, keepdims=True))
        alpha = jnp.exp(m_old - m_new)
        pr = jnp.exp(sc - m_new)
        l_ref[...] = alpha * l_ref[...] + jnp.sum(pr, axis=-1, keepdims=True)
        acc_ref[...] = alpha * acc_ref[...] + pv_of(pr.astype(BF16))
        m_ref[...] = m_new

    kk = kc_ref[...].astype(BF16)
    vv = vc_ref[...].astype(BF16)
    n = kk.shape[0]
    rh = lax.broadcasted_iota(jnp.int32, (rows, n), 0) & (heads - 1)
    ch = lax.broadcasted_iota(jnp.int32, (rows, n), 1) & (heads - 1)
    sc = jnp.where(rh == ch, _dot_nt(qmat, kk) * (hd ** -0.5), -jnp.inf)
    online_update(sc, lambda pr: _dot(pr, vv))

    @pl.when(p == pl.num_programs(1) - 1)
    def _():
        kn = kn_ref[...].astype(BF16).astype(F32)
        vn = vn_ref[...].astype(BF16).astype(F32)
        kn2 = jnp.concatenate([kn, kn], axis=0)
        vn2 = jnp.concatenate([vn, vn], axis=0)
        sc_new = jnp.sum(qmat.astype(F32) * kn2, axis=-1, keepdims=True) * (hd ** -0.5)
        online_update(sc_new, lambda pr: pr.astype(F32) * vn2)
        lam = _lambda(lq1_ref, lk1_ref, lq2_ref, lk2_ref, lam_init)
        on = acc_ref[...] / l_ref[...]
        o = on[:heads] - lam * on[heads:]
        o_ref[...] = _rms(o, sub_ref[...]) * (1.0 - lam_init)


def diff_attention_decode(q, k_new, v_new, cache_k, cache_v, layer, page_table, lam_params, subln,
                          *, heads, lam_init):
    bs, _, vd = q.shape
    hd = vd // 2
    assert heads & (heads - 1) == 0, "same-head mask uses a power-of-two head count"
    n_layers, n_phys, page, _, _ = cache_k.shape
    n_pages = page_table.shape[1]
    ck = cache_k.reshape(n_layers, n_phys, page * heads, vd)
    cv = cache_v.reshape(n_layers, n_phys, page * heads, vd)
    tok = pl.BlockSpec((None, heads, vd), lambda b, p, pt: (b, 0, 0))
    cspec = pl.BlockSpec((None, None, page * heads, vd), lambda b, p, pt: (layer, pt[b * n_pages + p], 0, 0))
    pspec = pl.BlockSpec((1, hd), lambda b, p, pt: (0, 0))
    return pl.pallas_call(
        functools.partial(_decode_kernel, heads=heads, hd=hd, lam_init=lam_init),
        grid_spec=pltpu.PrefetchScalarGridSpec(
            num_scalar_prefetch=1,
            grid=(bs, n_pages),
            in_specs=[tok, cspec, cspec, tok, tok, pspec, pspec, pspec, pspec,
                      pl.BlockSpec((1, vd), lambda b, p, pt: (0, 0))],
            out_specs=tok,
            scratch_shapes=[pltpu.VMEM((2 * heads, 1), F32), pltpu.VMEM((2 * heads, 1), F32),
                            pltpu.VMEM((2 * heads, vd), F32)],
        ),
        out_shape=jax.ShapeDtypeStruct((bs, heads, vd), F32),
        compiler_params=_cparams("parallel", "arbitrary"),
        name="diff_attention_decode",
    )(page_table.reshape(-1), q, ck, cv, k_new, v_new,
      *[p.reshape(1, hd) for p in lam_params], subln.reshape(1, vd))


def _log_sigmoid(x):
    return jnp.minimum(x, 0.0) - jnp.log1p(jnp.exp(-jnp.abs(x)))


def _gla_gate_kernel(x_ref, g_ref, w1_ref, w2_ref, b_ref, o_ref, *, gate_norm):
    h = _rms(x_ref[...], g_ref[...]).astype(BF16)
    low = _dot(h, w1_ref[...]).astype(BF16)
    gk = _dot(low, w2_ref[...]) + b_ref[...]
    o_ref[...] = _log_sigmoid(gk) / gate_norm


def gla_gate(x, gain, w1, w2, bias, *, tm, gate_norm):
    m, d = x.shape
    r = w1.shape[1]
    n = w2.shape[1]
    tm = min(tm, m)
    full = lambda shape: pl.BlockSpec(shape, lambda i: (0, 0))
    return pl.pallas_call(
        functools.partial(_gla_gate_kernel, gate_norm=gate_norm),
        grid=(m // tm,),
        in_specs=[pl.BlockSpec((tm, d), lambda i: (i, 0)), full((1, d)), full((d, r)), full((r, n)), full((1, n))],
        out_specs=pl.BlockSpec((tm, n), lambda i: (i, 0)),
        out_shape=jax.ShapeDtypeStruct((m, n), F32),
        compiler_params=_cparams("parallel"),
        name="gla_gate",
    )(x, gain.reshape(1, d), w1, w2, bias.reshape(1, n))


def _split3(x):
    hi = x.astype(BF16)
    r1 = x - hi.astype(F32)
    mid = r1.astype(BF16)
    lo = (r1 - mid.astype(F32)).astype(BF16)
    return hi, mid, lo


def _gla_chunk_kernel(q_ref, k_ref, v_ref, og_ref, la_ref, ng_ref, y_ref, s_ref, st_ref, *, chunk, dk):
    r = pl.program_id(2)

    @pl.when(r == 0)
    def _():
        st_ref[...] = jnp.zeros_like(st_ref)

    rows = q_ref.shape[0]
    ti = lax.broadcasted_iota(jnp.int32, (chunk, chunk), 0)
    si = lax.broadcasted_iota(jnp.int32, (chunk, chunk), 1)
    causal = si <= ti
    tri = causal.astype(BF16)
    outs = []
    for c in range(rows // chunk):
        sl = slice(c * chunk, (c + 1) * chunk)
        q = q_ref[sl, :] * (dk ** -0.5)
        k = k_ref[sl, :]
        v = v_ref[sl, :].astype(BF16)
        hi, mid, lo = _split3(la_ref[sl, :])
        b = _dot(tri, hi) + _dot(tri, mid) + _dot(tri, lo)
        b_last = b[chunk - 1:chunk, :]
        q_e = (q * jnp.exp(b)).astype(BF16)
        k_e = (k * jnp.exp(-b)).astype(BF16)
        k_tail = (k * jnp.exp(b_last - b)).astype(BF16)
        attn = jnp.where(causal, _dot_nt(q_e, k_e), 0.0).astype(BF16)
        st = st_ref[...]
        outs.append(_dot(attn, v) + _dot_nt(q_e, st.astype(BF16)))
        st_ref[...] = jnp.exp(b_last) * st + _dot_tn(v, k_tail)
    o = jnp.concatenate(outs, axis=0)
    y_ref[...] = (_rms(o, ng_ref[...]) * _silu(og_ref[...])).astype(BF16)

    @pl.when(r == pl.num_programs(2) - 1)
    def _():
        s_ref[...] = st_ref[...].T


def gla_prompt_core(proj, log_a, norm_g, *, batch, heads, rows, chunk):
    m = proj.shape[0]
    kd = log_a.shape[1]
    vd = (proj.shape[1] - 2 * kd) // 2
    dk, dv = kd // heads, vd // heads
    seq = m // batch
    rows = min(rows, seq)
    nr = seq // rows
    rowblk = lambda col0, width: pl.BlockSpec((rows, width), lambda b, h, r: (b * nr + r, col0 // width + h))
    return pl.pallas_call(
        functools.partial(_gla_chunk_kernel, chunk=chunk, dk=dk),
        grid=(batch, heads, nr),
        in_specs=[rowblk(0, dk), rowblk(kd, dk), rowblk(2 * kd, dv), rowblk(2 * kd + vd, dv),
                  pl.BlockSpec((rows, dk), lambda b, h, r: (b * nr + r, h)),
                  pl.BlockSpec((1, dv), lambda b, h, r: (0, 0))],
        out_specs=[pl.BlockSpec((rows, dv), lambda b, h, r: (b * nr + r, h)),
                   pl.BlockSpec((None, None, dk, dv), lambda b, h, r: (b, h, 0, 0))],
        out_shape=[jax.ShapeDtypeStruct((m, vd), BF16), jax.ShapeDtypeStruct((batch, heads, dk, dv), F32)],
        scratch_shapes=[pltpu.VMEM((dv, dk), F32)],
        compiler_params=_cparams("parallel", "parallel", "arbitrary"),
        name="gla_prompt_core",
    )(proj, proj, proj, proj, log_a, norm_g.reshape(1, dv))


def _gla_step_kernel(q_ref, kcol_ref, gcol_ref, v_ref, og_ref, s0_ref, ng_ref, y_ref, s_ref, *, dk):
    s_new = jnp.exp(gcol_ref[...]) * s0_ref[...] + kcol_ref[...] * v_ref[...]
    s_ref[...] = s_new
    q = jnp.broadcast_to(q_ref[...] * (dk ** -0.5), (8, dk)).astype(BF16)
    o = _dot(q, s_new.astype(BF16))[0:1, :]
    y_ref[...] = _rms(o, ng_ref[...]) * _silu(og_ref[...])


def gla_decode_step(q, k, log_a, v, og, state, norm_g):
    bs, heads, dk = q.shape
    dv = v.shape[-1]
    row = lambda w: pl.BlockSpec((None, None, 1, w), lambda b, h: (b, h, 0, 0))
    col = pl.BlockSpec((None, None, dk, 1), lambda b, h: (b, h, 0, 0))
    mat = pl.BlockSpec((None, None, dk, dv), lambda b, h: (b, h, 0, 0))
    y, s = pl.pallas_call(
        functools.partial(_gla_step_kernel, dk=dk),
        grid=(bs, heads),
        in_specs=[row(dk), col, col, row(dv), row(dv), mat, pl.BlockSpec((1, dv), lambda b, h: (0, 0))],
        out_specs=[row(dv), mat],
        out_shape=[jax.ShapeDtypeStruct((bs, heads, 1, dv), F32), jax.ShapeDtypeStruct((bs, heads, dk, dv), F32)],
        compiler_params=_cparams("parallel", "parallel"),
        name="gla_decode_step",
    )(q[:, :, None, :], k[..., None], log_a[..., None], v[:, :, None, :], og[:, :, None, :], state,
      norm_g.reshape(1, dv))
    return y.reshape(bs, heads * dv), s


def _pad_axis(a, axis, mult):
    pad = (-a.shape[axis]) % mult
    if pad == 0:
        return a
    widths = [(0, 0)] * a.ndim
    widths[axis] = (0, pad)
    return jnp.pad(a, widths)


def _forward(x_prompt, x_sample, cache_k, cache_v, state_gla, page_table, norm_sub, final_norm_g,
             w_ffn_gate, w_ffn_up, w_ffn_down, w_qkv_a, lambda_q1, lambda_k1, lambda_q2, lambda_k2,
             subln_a, w_o_a, w_proj_g, w_gk1, w_gk2, b_gk, norm_g, w_o_g, cfg):
    batch, seq, d = x_prompt.shape
    bs = x_sample.shape[0]
    depth = norm_sub.shape[0]
    heads, gheads = cfg["attn_heads"], cfg["gla_heads"]
    n_past = page_table.shape[1] * cache_k.shape[2]
    da = w_qkv_a.shape[2] // 3
    vd_attn = da // heads
    kd = w_gk2.shape[2]
    vd = (w_proj_g.shape[2] - 2 * kd) // 2
    dk, dv = kd // gheads, vd // gheads
    bs_pad = -(-bs // BF16_SUBLANES) * BF16_SUBLANES

    xp = x_prompt.reshape(batch * seq, d)
    xs = _pad_axis(x_sample.reshape(bs, d), 0, BF16_SUBLANES)

    tf = cfg["ffn_tf"]
    wg = _pad_axis(w_ffn_gate.astype(BF16), 3, tf)
    wu = _pad_axis(w_ffn_up.astype(BF16), 3, tf)
    wd = _pad_axis(w_ffn_down.astype(BF16), 2, tf)
    wqkv = w_qkv_a.astype(BF16)
    woa = w_o_a.astype(BF16)
    wproj = w_proj_g.astype(BF16)
    wog = w_o_g.astype(BF16)
    wgk1 = _pad_axis(w_gk1.astype(BF16), 2, LANES)
    wgk2 = _pad_axis(w_gk2.astype(BF16), 1, LANES)

    cos_p, sin_p = _rotary_tables(jnp.arange(seq), cfg["rot_dim"])
    cos_s, sin_s = _rotary_tables(jnp.full((bs_pad,), n_past, jnp.int32), cfg["rot_dim"])

    ffn = functools.partial(ffn_half, tm=cfg["ffn_tm"], tf=tf)
    proj_t = dict(tm=cfg["proj_tm"], tn=cfg["proj_tn"])

    k_p, v_p, k_s, v_s, st_p, st_s = [], [], [], [], [], []
    for i in range(depth):
        j = i // cfg["n_mixers"]
        xp = ffn(xp, norm_sub[i, 0], wg[i, 0], wu[i, 0], wd[i, 0])
        xs = ffn(xs, norm_sub[i, 0], wg[i, 0], wu[i, 0], wd[i, 0])
        if i % cfg["n_mixers"] == 0:
            lam_init = 0.8 - 0.6 * math.exp(-0.3 * i)
            lam_params = (lambda_q1[j], lambda_k1[j], lambda_q2[j], lambda_k2[j])
            qb, kb, vb, kf, vf = qkv_project(xp, norm_sub[i, 1], wqkv[j], cos_p, sin_p,
                                             rot_dim=cfg["rot_dim"], **proj_t)
            k_p.append(kf.reshape(batch, seq, heads, vd_attn))
            v_p.append(vf.reshape(batch, seq, heads, vd_attn))
            a = diff_attention_prompt(qb, kb, vb, lam_params, subln_a[j], batch=batch, heads=heads,
                                      t=cfg["attn_t"], lam_init=lam_init)
            xp = matmul_residual(a, woa[j], xp, **proj_t)

            qb, _, _, kf, vf = qkv_project(xs, norm_sub[i, 1], wqkv[j], cos_s, sin_s,
                                           rot_dim=cfg["rot_dim"], **proj_t)
            tok = lambda t: t[:bs].reshape(bs, heads, vd_attn)
            k_s.append(tok(kf).reshape(bs, 1, heads, vd_attn))
            v_s.append(tok(vf).reshape(bs, 1, heads, vd_attn))
            o = diff_attention_decode(tok(qb).astype(F32), tok(kf), tok(vf), cache_k, cache_v, j, page_table,
                                      lam_params, subln_a[j], heads=heads, lam_init=lam_init)
            a = _pad_axis(o.reshape(bs, da), 0, BF16_SUBLANES).astype(BF16)
            xs = matmul_residual(a, woa[j], xs, **proj_t)
        else:
            gate = functools.partial(gla_gate, tm=cfg["proj_tm"], gate_norm=cfg["gla_gate_norm"])
            proj = norm_matmul(xp, norm_sub[i, 1], wproj[j], **proj_t)
            log_a = gate(xp, norm_sub[i, 1], wgk1[j], wgk2[j], b_gk[j])
            y, s_fin = gla_prompt_core(proj, log_a, norm_g[j], batch=batch, heads=gheads,
                                       rows=cfg["gla_rows"], chunk=cfg["gla_chunk"])
            st_p.append(s_fin)
            xp = matmul_residual(y, wog[j], xp, **proj_t)

            proj = norm_matmul(xs, norm_sub[i, 1], wproj[j], **proj_t)[:bs]
            log_a = gate(xs, norm_sub[i, 1], wgk1[j], wgk2[j], b_gk[j])[:bs]
            y, s_new = gla_decode_step(
                proj[:, :kd].reshape(bs, gheads, dk), proj[:, kd:2 * kd].reshape(bs, gheads, dk),
                log_a.reshape(bs, gheads, dk), proj[:, 2 * kd:2 * kd + vd].reshape(bs, gheads, dv),
                proj[:, 2 * kd + vd:].reshape(bs, gheads, dv), state_gla[j], norm_g[j])
            st_s.append(s_new)
            xs = matmul_residual(_pad_axis(y, 0, BF16_SUBLANES).astype(BF16), wog[j], xs, **proj_t)
        xp = ffn(xp, norm_sub[i, 2], wg[i, 1], wu[i, 1], wd[i, 1])
        xs = ffn(xs, norm_sub[i, 2], wg[i, 1], wu[i, 1], wd[i, 1])

    yp = final_norm(xp, final_norm_g, tm=cfg["proj_tm"]).reshape(batch, seq, d)
    ys = final_norm(xs, final_norm_g, tm=cfg["proj_tm"])[:bs].reshape(bs, 1, d)
    return (yp, ys, jnp.stack(k_p), jnp.stack(v_p), jnp.stack(st_p), jnp.stack(k_s), jnp.stack(v_s),
            jnp.stack(st_s))


def kernel(x_prompt, x_sample, cache_k, cache_v, state_gla, page_table, norm_sub, final_norm, w_ffn_gate, w_ffn_up, w_ffn_down, w_qkv_a, lambda_q1, lambda_k1, lambda_q2, lambda_k2, subln_a, w_o_a, w_proj_g, w_gk1, w_gk2, b_gk, norm_g, w_o_g):
    return _forward(x_prompt, x_sample, cache_k, cache_v, state_gla, page_table, norm_sub, final_norm,
                    w_ffn_gate, w_ffn_up, w_ffn_down, w_qkv_a, lambda_q1, lambda_k1, lambda_q2, lambda_k2,
                    subln_a, w_o_a, w_proj_g, w_gk1, w_gk2, b_gk, norm_g, w_o_g, CFG)
```

```python
import functools
import math

import jax
import jax.numpy as jnp
from jax import lax
from jax.experimental import pallas as pl
from jax.experimental.pallas import tpu as pltpu

F32 = jnp.float32
BF16 = jnp.bfloat16

RMS_EPS = 1e-6
ROPE_THETA = 500000.0
FFN_RESIDUAL_SCALE = 0.5
LOG2_E = 1.4426950408889634
LANES = 128
BF16_SUBLANES = 16
VMEM_LIMIT_BYTES = 56 * 1024 * 1024

CFG = dict(
    attn_heads=8, gla_heads=4, rot_dim=32, gla_gate_norm=16.0, gla_chunk=64, n_mixers=2,
    ffn_tm=512, ffn_tf=512, proj_tm=1024, proj_tn=1024, qkv_tm=1024, qkv_tn=512, out_tm=512,
    attn_tq=512, attn_tk=512, decode_pages=8, gla_rows=256,
)


def _cparams(*sem):
    return pltpu.CompilerParams(dimension_semantics=sem, vmem_limit_bytes=VMEM_LIMIT_BYTES)


def _rms(x, gain):
    return x * lax.rsqrt(jnp.mean(x * x, axis=-1, keepdims=True) + RMS_EPS) * gain


def _silu(x):
    return x * jax.nn.sigmoid(x)


def _dot(a, b):
    return jnp.dot(a, b, preferred_element_type=F32)


def _dot_nt(a, b):
    return lax.dot_general(a, b, (((1,), (1,)), ((), ())), preferred_element_type=F32)


def _dot_tn(a, b):
    return lax.dot_general(a, b, (((0,), (0,)), ((), ())), preferred_element_type=F32)


def _layer_spec(block, index_map, lead):
    return pl.BlockSpec((None,) * len(lead) + block, lambda *g: tuple(lead) + tuple(index_map(*g)))


def _ffn_kernel(x_ref, g_ref, wg_ref, wu_ref, wd_ref, o_ref, h_ref):
    @pl.when(pl.program_id(1) == 0)
    def _():
        x = x_ref[...]
        h_ref[...] = _rms(x, g_ref[...]).astype(BF16)
        o_ref[...] = x

    h = h_ref[...]
    a = (_silu(_dot(h, wg_ref[...])) * (FFN_RESIDUAL_SCALE * _dot(h, wu_ref[...]))).astype(BF16)
    o_ref[...] += _dot(a, wd_ref[...])


def ffn_half(x, gain, wg, wu, wd, lead, *, tm, tf):
    m, d = x.shape
    ffp = wg.shape[-1]
    tm = min(tm, m)
    return pl.pallas_call(
        _ffn_kernel,
        grid=(m // tm, ffp // tf),
        in_specs=[
            pl.BlockSpec((tm, d), lambda i, f: (i, 0)),
            pl.BlockSpec((1, d), lambda i, f: (0, 0)),
            _layer_spec((d, tf), lambda i, f: (0, f), lead),
            _layer_spec((d, tf), lambda i, f: (0, f), lead),
            _layer_spec((tf, d), lambda i, f: (f, 0), lead),
        ],
        out_specs=pl.BlockSpec((tm, d), lambda i, f: (i, 0)),
        out_shape=jax.ShapeDtypeStruct((m, d), F32),
        scratch_shapes=[pltpu.VMEM((tm, d), BF16)],
        compiler_params=_cparams("parallel", "arbitrary"),
        name="ffn_half",
    )(x, gain.reshape(1, d), wg, wu, wd)


def _norm_matmul_kernel(x_ref, g_ref, w_ref, o_ref, h_ref):
    @pl.when(pl.program_id(1) == 0)
    def _():
        h_ref[...] = _rms(x_ref[...], g_ref[...]).astype(BF16)

    o_ref[...] = _dot(h_ref[...], w_ref[...])


def norm_matmul(x, gain, w, lead, *, tm, tn):
    m, d = x.shape
    n = w.shape[-1]
    tm = min(tm, m)
    return pl.pallas_call(
        _norm_matmul_kernel,
        grid=(m // tm, n // tn),
        in_specs=[
            pl.BlockSpec((tm, d), lambda i, j: (i, 0)),
            pl.BlockSpec((1, d), lambda i, j: (0, 0)),
            _layer_spec((d, tn), lambda i, j: (0, j), lead),
        ],
        out_specs=pl.BlockSpec((tm, tn), lambda i, j: (i, j)),
        out_shape=jax.ShapeDtypeStruct((m, n), F32),
        scratch_shapes=[pltpu.VMEM((tm, d), BF16)],
        compiler_params=_cparams("parallel", "arbitrary"),
        name="norm_matmul",
    )(x, gain.reshape(1, d), w)


def _matmul_residual_kernel(a_ref, w_ref, x_ref, o_ref):
    o_ref[...] = x_ref[...] + _dot(a_ref[...], w_ref[...])


def matmul_residual(a, w, x, lead, *, tm):
    m, k = a.shape
    n = w.shape[-1]
    tm = min(tm, m)
    return pl.pallas_call(
        _matmul_residual_kernel,
        grid=(m // tm,),
        in_specs=[
            pl.BlockSpec((tm, k), lambda i: (i, 0)),
            _layer_spec((k, n), lambda i: (0, 0), lead),
            pl.BlockSpec((tm, n), lambda i: (i, 0)),
        ],
        out_specs=pl.BlockSpec((tm, n), lambda i: (i, 0)),
        out_shape=jax.ShapeDtypeStruct((m, n), F32),
        compiler_params=_cparams("parallel"),
        name="matmul_residual",
    )(a, w, x)


def _final_norm_kernel(x_ref, g_ref, o_ref):
    o_ref[...] = _rms(x_ref[...], g_ref[...])


def final_norm(x, gain, *, tm):
    m, d = x.shape
    tm = min(tm, m)
    return pl.pallas_call(
        _final_norm_kernel,
        grid=(m // tm,),
        in_specs=[pl.BlockSpec((tm, d), lambda i: (i, 0)), pl.BlockSpec((1, d), lambda i: (0, 0))],
        out_specs=pl.BlockSpec((tm, d), lambda i: (i, 0)),
        out_shape=jax.ShapeDtypeStruct((m, d), F32),
        compiler_params=_cparams("parallel"),
        name="final_norm",
    )(x, gain.reshape(1, d))


def _rotary_tables(pos, rot_dim):
    inv_freq = jnp.power(ROPE_THETA, -jnp.arange(0, rot_dim, 2, dtype=F32) / rot_dim)
    ang = pos.astype(F32)[:, None] * inv_freq[None, :]
    cos, sin = jnp.cos(ang), jnp.sin(ang)
    n = pos.shape[0]
    cos_t = jnp.concatenate([cos, cos, jnp.ones((n, LANES - rot_dim), F32)], axis=-1)
    sin_t = jnp.concatenate([-sin, sin, jnp.zeros((n, LANES - rot_dim), F32)], axis=-1)
    return cos_t, sin_t


def _rotate(t, cos_t, sin_t, rot_dim):
    half = rot_dim // 2
    lane = lax.broadcasted_iota(jnp.int32, (t.shape[0], LANES), 1)
    segs = []
    for s in range(t.shape[1] // LANES):
        xs = t[:, s * LANES:(s + 1) * LANES]
        up = pltpu.roll(xs, LANES - half, 1)
        down = pltpu.roll(xs, half, 1)
        partner = jnp.where(lane < half, up, down)
        segs.append(jnp.where(lane < rot_dim, xs * cos_t + partner * sin_t, xs))
    return jnp.concatenate(segs, axis=1)


def _qkv_kernel(x_ref, g_ref, wq_ref, wk_ref, wv_ref, cos_ref, sin_ref,
                qb_ref, kf_ref, vf_ref, h_ref, *, rot_dim):
    @pl.when(pl.program_id(1) == 0)
    def _():
        h_ref[...] = _rms(x_ref[...], g_ref[...]).astype(BF16)

    h = h_ref[...]
    cos_t, sin_t = cos_ref[...], sin_ref[...]
    qb_ref[...] = _rotate(_dot(h, wq_ref[...]), cos_t, sin_t, rot_dim).astype(BF16)
    kf_ref[...] = _rotate(_dot(h, wk_ref[...]), cos_t, sin_t, rot_dim)
    vf_ref[...] = _dot(h, wv_ref[...])


def qkv_project(x, gain, w_qkv, lead, cos_t, sin_t, *, tm, tn, rot_dim):
    m, d = x.shape
    da = w_qkv.shape[-1] // 3
    tm = min(tm, m)
    nj = da // tn
    ntab = cos_t.shape[0] // tm
    wspec = lambda sec: _layer_spec((d, tn), lambda i, j: (0, sec * nj + j), lead)
    tspec = pl.BlockSpec((tm, LANES), lambda i, j: (i % ntab, 0))
    ospec = pl.BlockSpec((tm, tn), lambda i, j: (i, j))
    return pl.pallas_call(
        functools.partial(_qkv_kernel, rot_dim=rot_dim),
        grid=(m // tm, nj),
        in_specs=[
            pl.BlockSpec((tm, d), lambda i, j: (i, 0)),
            pl.BlockSpec((1, d), lambda i, j: (0, 0)),
            wspec(0), wspec(1), wspec(2), tspec, tspec,
        ],
        out_specs=[ospec] * 3,
        out_shape=[jax.ShapeDtypeStruct((m, da), BF16)] + [jax.ShapeDtypeStruct((m, da), F32)] * 2,
        scratch_shapes=[pltpu.VMEM((tm, d), BF16)],
        compiler_params=_cparams("parallel", "arbitrary"),
        name="qkv_project",
    )(x, gain.reshape(1, d), w_qkv, w_qkv, w_qkv, cos_t, sin_t)


def _lambda(lq1_ref, lk1_ref, lq2_ref, lk2_ref, lam_init):
    e1 = jnp.exp(jnp.sum(lq1_ref[...] * lk1_ref[...], axis=-1, keepdims=True))
    e2 = jnp.exp(jnp.sum(lq2_ref[...] * lk2_ref[...], axis=-1, keepdims=True))
    return e1 - e2 + lam_init


def _attn_kernel(q_ref, k_ref, v_ref, lq1_ref, lk1_ref, lq2_ref, lk2_ref, sub_ref,
                 o_ref, kb_ref, vb_ref, m_ref, l_ref, acc_ref, *, tq, tk, hd, lam_init):
    i = pl.program_id(2)
    c_scale = (hd ** -0.5) * LOG2_E
    vd = 2 * hd

    @pl.when(i == 0)
    def _():
        kb_ref[...] = k_ref[...].astype(BF16)
        vb_ref[...] = v_ref[...].astype(BF16)

    m_ref[...] = jnp.full_like(m_ref, -jnp.inf)
    l_ref[...] = jnp.zeros_like(l_ref)
    acc_ref[...] = jnp.zeros_like(acc_ref)
    q = q_ref[...]

    def kv_block(j, masked):
        start = pl.multiple_of(j * tk, tk)
        k = kb_ref[pl.ds(start, tk), :]
        v = vb_ref[pl.ds(start, tk), :]
        if masked:
            row = lax.broadcasted_iota(jnp.int32, (tq, tk), 0) + i * tq
            col = lax.broadcasted_iota(jnp.int32, (tq, tk), 1) + j * tk
            visible = col <= row
        for c in range(2):
            sc = _dot_nt(q[:, c * hd:(c + 1) * hd], k[:, c * hd:(c + 1) * hd]) * c_scale
            if masked:
                sc = jnp.where(visible, sc, -jnp.inf)
            m_old = m_ref[c]
            m_new = jnp.maximum(m_old, jnp.max(sc, axis=-1, keepdims=True))
            alpha = jnp.exp2(m_old - m_new)
            p = jnp.exp2(sc - jnp.tile(m_new, (1, tk // LANES)))
            l_ref[c] = alpha * l_ref[c] + jnp.sum(p, axis=-1, keepdims=True)
            acc_ref[c] = jnp.tile(alpha, (1, vd // LANES)) * acc_ref[c] + _dot(p.astype(BF16), v)
            m_ref[c] = m_new

    n_below = (i * tq) // tk

    def body(j, carry):
        kv_block(j, False)
        return carry

    lax.fori_loop(0, n_below, body, 0)
    for dj in range(tq // tk):
        kv_block(n_below + dj, True)

    lam = _lambda(lq1_ref, lk1_ref, lq2_ref, lk2_ref, lam_init)
    l0 = jnp.tile(l_ref[0], (1, vd // LANES))
    l1 = jnp.tile(l_ref[1], (1, vd // LANES))
    o = acc_ref[0] / l0 - lam * (acc_ref[1] / l1)
    o_ref[...] = (_rms(o, sub_ref[...]) * (1.0 - lam_init)).astype(BF16)


def diff_attention_prompt(q, k, v, lam_params, subln, *, batch, heads, tq, tk, lam_init):
    m, da = q.shape
    vd = da // heads
    hd = vd // 2
    seq = m // batch
    tq, tk = min(tq, seq), min(tk, seq)
    assert tq % tk == 0 and seq % tq == 0
    nq = seq // tq
    qspec = pl.BlockSpec((tq, vd), lambda b, h, i: (b * nq + i, h))
    kspec = pl.BlockSpec((seq, vd), lambda b, h, i: (b, h))
    pspec = pl.BlockSpec((1, hd), lambda b, h, i: (0, 0))
    return pl.pallas_call(
        functools.partial(_attn_kernel, tq=tq, tk=tk, hd=hd, lam_init=lam_init),
        grid=(batch, heads, nq),
        in_specs=[qspec, kspec, kspec, pspec, pspec, pspec, pspec, pl.BlockSpec((1, vd), lambda b, h, i: (0, 0))],
        out_specs=qspec,
        out_shape=jax.ShapeDtypeStruct((m, da), BF16),
        scratch_shapes=[pltpu.VMEM((seq, vd), BF16), pltpu.VMEM((seq, vd), BF16),
                        pltpu.VMEM((2, tq, LANES), F32), pltpu.VMEM((2, tq, LANES), F32),
                        pltpu.VMEM((2, tq, vd), F32)],
        compiler_params=_cparams("parallel", "parallel", "arbitrary"),
        name="diff_attention_prompt",
    )(q, k, v, *[p.reshape(1, hd) for p in lam_params], subln.reshape(1, vd))


def _decode_kernel(pt_ref, q_ref, *refs, heads, hd, lam_init, pps):
    kc_refs, vc_refs = refs[:pps], refs[pps:2 * pps]
    (kn_ref, vn_ref, lq1_ref, lk1_ref, lq2_ref, lk2_ref, sub_ref, o_ref, m_ref, l_ref, acc_ref) = refs[2 * pps:]
    p = pl.program_id(1)
    rows = 2 * heads
    c_scale = (hd ** -0.5) * LOG2_E

    @pl.when(p == 0)
    def _():
        m_ref[...] = jnp.full_like(m_ref, -jnp.inf)
        l_ref[...] = jnp.zeros_like(l_ref)
        acc_ref[...] = jnp.zeros_like(acc_ref)

    q = q_ref[...]
    lane = lax.broadcasted_iota(jnp.int32, q.shape, 1)
    qmat = jnp.concatenate([jnp.where(lane < hd, q, 0.0), jnp.where(lane >= hd, q, 0.0)], axis=0).astype(BF16)

    def online_update(scs, pv_of):
        m_old = m_ref[...]
        m_new = m_old
        for sc in scs:
            m_new = jnp.maximum(m_new, jnp.max(sc, axis=-1, keepdims=True))
        alpha = jnp.exp2(m_old - m_new)
        l_new = alpha * l_ref[...]
        acc = alpha * acc_ref[...]
        for r, sc in enumerate(scs):
            pr = jnp.exp2(sc - m_new)
            l_new = l_new + jnp.sum(pr, axis=-1, keepdims=True)
            acc = acc + pv_of(r, pr.astype(BF16))
        l_ref[...] = l_new
        acc_ref[...] = acc
        m_ref[...] = m_new

    n = kc_refs[0].shape[0]
    rh = lax.broadcasted_iota(jnp.int32, (rows, n), 0) & (heads - 1)
    ch = lax.broadcasted_iota(jnp.int32, (rows, n), 1) & (heads - 1)
    same_head = rh == ch
    scs = [jnp.where(same_head, _dot_nt(qmat, kc[...].astype(BF16)) * c_scale, -jnp.inf) for kc in kc_refs]
    online_update(scs, lambda r, pr: _dot(pr, vc_refs[r][...].astype(BF16)))

    @pl.when(p == pl.num_programs(1) - 1)
    def _():
        kn = kn_ref[...].astype(BF16).astype(F32)
        vn = vn_ref[...].astype(BF16).astype(F32)
        kn2 = jnp.concatenate([kn, kn], axis=0)
        vn2 = jnp.concatenate([vn, vn], axis=0)
        sc_new = jnp.sum(qmat.astype(F32) * kn2, axis=-1, keepdims=True) * c_scale
        online_update([sc_new], lambda r, pr: pr.astype(F32) * vn2)
        lam = _lambda(lq1_ref, lk1_ref, lq2_ref, lk2_ref, lam_init)
        on = acc_ref[...] / l_ref[...]
        o = on[:heads] - lam * on[heads:]
        o_ref[...] = _rms(o, sub_ref[...]) * (1.0 - lam_init)


def diff_attention_decode(q, k_new, v_new, cache_k, cache_v, layer, page_table, lam_params, subln,
                          *, heads, lam_init, pages_per_step):
    bs, _, vd = q.shape
    hd = vd // 2
    assert heads & (heads - 1) == 0, "same-head mask uses a power-of-two head count"
    n_layers, n_phys, page, _, _ = cache_k.shape
    n_pages = page_table.shape[1]
    pps = math.gcd(n_pages, pages_per_step)
    ck = cache_k.reshape(n_layers, n_phys, page * heads, vd)
    cv = cache_v.reshape(n_layers, n_phys, page * heads, vd)
    tok = pl.BlockSpec((None, heads, vd), lambda b, p, pt: (b, 0, 0))
    cspec = lambda r: pl.BlockSpec((None, None, page * heads, vd),
                                   lambda b, p, pt: (layer, pt[b * n_pages + p * pps + r], 0, 0))
    pspec = pl.BlockSpec((1, hd), lambda b, p, pt: (0, 0))
    return pl.pallas_call(
        functools.partial(_decode_kernel, heads=heads, hd=hd, lam_init=lam_init, pps=pps),
        grid_spec=pltpu.PrefetchScalarGridSpec(
            num_scalar_prefetch=1,
            grid=(bs, n_pages // pps),
            in_specs=[tok] + [cspec(r) for r in range(pps)] * 2 + [tok, tok, pspec, pspec, pspec, pspec,
                      pl.BlockSpec((1, vd), lambda b, p, pt: (0, 0))],
            out_specs=tok,
            scratch_shapes=[pltpu.VMEM((2 * heads, 1), F32), pltpu.VMEM((2 * heads, 1), F32),
                            pltpu.VMEM((2 * heads, vd), F32)],
        ),
        out_shape=jax.ShapeDtypeStruct((bs, heads, vd), F32),
        compiler_params=_cparams("parallel", "arbitrary"),
        name="diff_attention_decode",
    )(page_table.reshape(-1), q, *([ck] * pps), *([cv] * pps), k_new, v_new,
      *[p.reshape(1, hd) for p in lam_params], subln.reshape(1, vd))


def _log_sigmoid(x):
    return jnp.minimum(x, 0.0) - jnp.log1p(jnp.exp(-jnp.abs(x)))


def _gla_gate_kernel(x_ref, g_ref, w1_ref, w2_ref, b_ref, o_ref, *, gate_norm):
    h = _rms(x_ref[...], g_ref[...]).astype(BF16)
    low = _dot(h, w1_ref[...]).astype(BF16)
    gk = _dot(low, w2_ref[...]) + b_ref[...]
    o_ref[...] = _log_sigmoid(gk) / gate_norm


def gla_gate(x, gain, w1, w2, bias, *, tm, gate_norm):
    m, d = x.shape
    r = w1.shape[1]
    n = w2.shape[1]
    tm = min(tm, m)
    full = lambda shape: pl.BlockSpec(shape, lambda i: (0, 0))
    return pl.pallas_call(
        functools.partial(_gla_gate_kernel, gate_norm=gate_norm),
        grid=(m // tm,),
        in_specs=[pl.BlockSpec((tm, d), lambda i: (i, 0)), full((1, d)), full((d, r)), full((r, n)), full((1, n))],
        out_specs=pl.BlockSpec((tm, n), lambda i: (i, 0)),
        out_shape=jax.ShapeDtypeStruct((m, n), F32),
        compiler_params=_cparams("parallel"),
        name="gla_gate",
    )(x, gain.reshape(1, d), w1, w2, bias.reshape(1, n))


def _split3(x):
    hi = x.astype(BF16)
    r1 = x - hi.astype(F32)
    mid = r1.astype(BF16)
    lo = (r1 - mid.astype(F32)).astype(BF16)
    return hi, mid, lo


def _gla_chunk_kernel(q_ref, k_ref, v_ref, og_ref, la_ref, ng_ref, y_ref, s_ref, st_ref, *, chunk, heads):
    r = pl.program_id(1)

    @pl.when(r == 0)
    def _():
        st_ref[...] = jnp.zeros_like(st_ref)

    rows = q_ref.shape[0]
    dk = q_ref.shape[1] // heads
    dv = v_ref.shape[1] // heads
    ti = lax.broadcasted_iota(jnp.int32, (chunk, chunk), 0)
    si = lax.broadcasted_iota(jnp.int32, (chunk, chunk), 1)
    causal = si <= ti
    tri = causal.astype(BF16)
    ng = ng_ref[...]
    for c in range(rows // chunk):
        sl = slice(c * chunk, (c + 1) * chunk)
        for h in range(heads):
            ks = slice(h * dk, (h + 1) * dk)
            vs = slice(h * dv, (h + 1) * dv)
            q = q_ref[sl, ks] * (dk ** -0.5)
            k = k_ref[sl, ks]
            v = v_ref[sl, vs].astype(BF16)
            hi, mid, lo = _split3(la_ref[sl, ks])
            b = _dot(tri, hi) + _dot(tri, mid) + _dot(tri, lo)
            b_last = b[chunk - 1:chunk, :]
            q_e = (q * jnp.exp(b)).astype(BF16)
            k_e = (k * jnp.exp(-b)).astype(BF16)
            k_tail = (k * jnp.exp(b_last - b)).astype(BF16)
            attn = jnp.where(causal, _dot_nt(q_e, k_e), 0.0).astype(BF16)
            st = st_ref[h]
            o = _dot(attn, v) + _dot_nt(q_e, st.astype(BF16))
            st_ref[h] = jnp.exp(b_last) * st + _dot_tn(v, k_tail)
            y_ref[sl, vs] = (_rms(o, ng) * _silu(og_ref[sl, vs])).astype(BF16)

    @pl.when(r == pl.num_programs(1) - 1)
    def _():
        for h in range(heads):
            s_ref[h] = st_ref[h].T


def gla_prompt_core(proj, log_a, norm_g, *, batch, heads, rows, chunk):
    m = proj.shape[0]
    kd = log_a.shape[1]
    vd = (proj.shape[1] - 2 * kd) // 2
    dk, dv = kd // heads, vd // heads
    seq = m // batch
    rows = min(rows, seq)
    nr = seq // rows
    rowblk = lambda col0, width: pl.BlockSpec((rows, width), lambda b, r: (b * nr + r, col0 // width))
    return pl.pallas_call(
        functools.partial(_gla_chunk_kernel, chunk=chunk, heads=heads),
        grid=(batch, nr),
        in_specs=[rowblk(0, kd), rowblk(kd, kd), rowblk(2 * kd, vd), rowblk(2 * kd + vd, vd),
                  pl.BlockSpec((rows, kd), lambda b, r: (b * nr + r, 0)),
                  pl.BlockSpec((1, dv), lambda b, r: (0, 0))],
        out_specs=[pl.BlockSpec((rows, vd), lambda b, r: (b * nr + r, 0)),
                   pl.BlockSpec((None, heads, dk, dv), lambda b, r: (b, 0, 0, 0))],
        out_shape=[jax.ShapeDtypeStruct((m, vd), BF16), jax.ShapeDtypeStruct((batch, heads, dk, dv), F32)],
        scratch_shapes=[pltpu.VMEM((heads, dv, dk), F32)],
        compiler_params=_cparams("parallel", "arbitrary"),
        name="gla_prompt_core",
    )(proj, proj, proj, proj, log_a, norm_g.reshape(1, dv))


def _gla_step_kernel(q_ref, kcol_ref, gcol_ref, v_ref, og_ref, s0_ref, ng_ref, y_ref, s_ref, *, dk):
    s_new = jnp.exp(gcol_ref[...]) * s0_ref[...] + kcol_ref[...] * v_ref[...]
    s_ref[...] = s_new
    q = jnp.broadcast_to(q_ref[...] * (dk ** -0.5), (8, dk)).astype(BF16)
    o = _dot(q, s_new.astype(BF16))[0:1, :]
    y_ref[...] = _rms(o, ng_ref[...]) * _silu(og_ref[...])


def gla_decode_step(q, k, log_a, v, og, state, norm_g):
    bs, heads, dk = q.shape
    dv = v.shape[-1]
    row = lambda w: pl.BlockSpec((None, None, 1, w), lambda b, h: (b, h, 0, 0))
    col = pl.BlockSpec((None, None, dk, 1), lambda b, h: (b, h, 0, 0))
    mat = pl.BlockSpec((None, None, dk, dv), lambda b, h: (b, h, 0, 0))
    y, s = pl.pallas_call(
        functools.partial(_gla_step_kernel, dk=dk),
        grid=(bs, heads),
        in_specs=[row(dk), col, col, row(dv), row(dv), mat, pl.BlockSpec((1, dv), lambda b, h: (0, 0))],
        out_specs=[row(dv), mat],
        out_shape=[jax.ShapeDtypeStruct((bs, heads, 1, dv), F32), jax.ShapeDtypeStruct((bs, heads, dk, dv), F32)],
        compiler_params=_cparams("parallel", "parallel"),
        name="gla_decode_step",
    )(q[:, :, None, :], k[..., None], log_a[..., None], v[:, :, None, :], og[:, :, None, :], state,
      norm_g.reshape(1, dv))
    return y.reshape(bs, heads * dv), s


def _pad_axis(a, axis, mult):
    pad = (-a.shape[axis]) % mult
    if pad == 0:
        return a
    widths = [(0, 0)] * a.ndim
    widths[axis] = (0, pad)
    return jnp.pad(a, widths)


def _forward(x_prompt, x_sample, cache_k, cache_v, state_gla, page_table, norm_sub, final_norm_g,
             w_ffn_gate, w_ffn_up, w_ffn_down, w_qkv_a, lambda_q1, lambda_k1, lambda_q2, lambda_k2,
             subln_a, w_o_a, w_proj_g, w_gk1, w_gk2, b_gk, norm_g, w_o_g, cfg):
    batch, seq, d = x_prompt.shape
    bs = x_sample.shape[0]
    depth = norm_sub.shape[0]
    heads, gheads = cfg["attn_heads"], cfg["gla_heads"]
    n_past = page_table.shape[1] * cache_k.shape[2]
    da = w_qkv_a.shape[2] // 3
    vd_attn = da // heads
    kd = w_gk2.shape[2]
    vd = (w_proj_g.shape[2] - 2 * kd) // 2
    dk, dv = kd // gheads, vd // gheads
    bs_pad = -(-bs // BF16_SUBLANES) * BF16_SUBLANES

    xp = x_prompt.reshape(batch * seq, d)
    xs = _pad_axis(x_sample.reshape(bs, d), 0, BF16_SUBLANES)

    tf = cfg["ffn_tf"]
    wg = _pad_axis(w_ffn_gate.astype(BF16), 3, tf)
    wu = _pad_axis(w_ffn_up.astype(BF16), 3, tf)
    wd = _pad_axis(w_ffn_down.astype(BF16), 2, tf)
    wqkv = w_qkv_a.astype(BF16)
    woa = w_o_a.astype(BF16)
    wproj = w_proj_g.astype(BF16)
    wog = w_o_g.astype(BF16)
    wgk1 = _pad_axis(w_gk1.astype(BF16), 2, LANES)
    wgk2 = _pad_axis(w_gk2.astype(BF16), 1, LANES)

    cos_p, sin_p = _rotary_tables(jnp.arange(seq), cfg["rot_dim"])
    cos_s, sin_s = _rotary_tables(jnp.full((bs_pad,), n_past, jnp.int32), cfg["rot_dim"])

    ffn = lambda x, i, half: ffn_half(x, norm_sub[i, 2 * half], wg, wu, wd, (i, half), tm=cfg["ffn_tm"], tf=tf)
    proj_t = dict(tm=cfg["proj_tm"], tn=cfg["proj_tn"])
    qkv_t = dict(tm=cfg["qkv_tm"], tn=cfg["qkv_tn"], rot_dim=cfg["rot_dim"])

    k_p, v_p, k_s, v_s, st_p, st_s = [], [], [], [], [], []
    for i in range(depth):
        j = i // cfg["n_mixers"]
        xp = ffn(xp, i, 0)
        xs = ffn(xs, i, 0)
        if i % cfg["n_mixers"] == 0:
            lam_init = 0.8 - 0.6 * math.exp(-0.3 * i)
            lam_params = (lambda_q1[j], lambda_k1[j], lambda_q2[j], lambda_k2[j])
            qb, kf, vf = qkv_project(xp, norm_sub[i, 1], wqkv, (j,), cos_p, sin_p, **qkv_t)
            k_p.append(kf.reshape(batch, seq, heads, vd_attn))
            v_p.append(vf.reshape(batch, seq, heads, vd_attn))
            a = diff_attention_prompt(qb, kf, vf, lam_params, subln_a[j], batch=batch, heads=heads,
                                      tq=cfg["attn_tq"], tk=cfg["attn_tk"], lam_init=lam_init)
            xp = matmul_residual(a, woa, xp, (j,), tm=cfg["out_tm"])

            qb, kf, vf = qkv_project(xs, norm_sub[i, 1], wqkv, (j,), cos_s, sin_s, **qkv_t)
            tok = lambda t: t[:bs].reshape(bs, heads, vd_attn)
            k_s.append(tok(kf).reshape(bs, 1, heads, vd_attn))
            v_s.append(tok(vf).reshape(bs, 1, heads, vd_attn))
            o = diff_attention_decode(tok(qb).astype(F32), tok(kf), tok(vf), cache_k, cache_v, j, page_table,
                                      lam_params, subln_a[j], heads=heads, lam_init=lam_init,
                                      pages_per_step=cfg["decode_pages"])
            a = _pad_axis(o.reshape(bs, da), 0, BF16_SUBLANES).astype(BF16)
            xs = matmul_residual(a, woa, xs, (j,), tm=cfg["out_tm"])
        else:
            gate = functools.partial(gla_gate, tm=cfg["proj_tm"], gate_norm=cfg["gla_gate_norm"])
            proj = norm_matmul(xp, norm_sub[i, 1], wproj, (j,), **proj_t)
            log_a = gate(xp, norm_sub[i, 1], wgk1[j], wgk2[j], b_gk[j])
            y, s_fin = gla_prompt_core(proj, log_a, norm_g[j], batch=batch, heads=gheads,
                                       rows=cfg["gla_rows"], chunk=cfg["gla_chunk"])
            st_p.append(s_fin)
            xp = matmul_residual(y, wog, xp, (j,), tm=cfg["out_tm"])

            proj = norm_matmul(xs, norm_sub[i, 1], wproj, (j,), **proj_t)[:bs]
            log_a = gate(xs, norm_sub[i, 1], wgk1[j], wgk2[j], b_gk[j])[:bs]
            y, s_new = gla_decode_step(
                proj[:, :kd].reshape(bs, gheads, dk), proj[:, kd:2 * kd].reshape(bs, gheads, dk),
                log_a.reshape(bs, gheads, dk), proj[:, 2 * kd:2 * kd + vd].reshape(bs, gheads, dv),
                proj[:, 2 * kd + vd:].reshape(bs, gheads, dv), state_gla[j], norm_g[j])
            st_s.append(s_new)
            xs = matmul_residual(_pad_axis(y, 0, BF16_SUBLANES).astype(BF16), wog, xs, (j,), tm=cfg["out_tm"])
        xp = ffn(xp, i, 1)
        xs = ffn(xs, i, 1)

    yp = final_norm(xp, final_norm_g, tm=cfg["out_tm"]).reshape(batch, seq, d)
    ys = final_norm(xs, final_norm_g, tm=cfg["out_tm"])[:bs].reshape(bs, 1, d)
    return (yp, ys, jnp.stack(k_p), jnp.stack(v_p), jnp.stack(st_p), jnp.stack(k_s), jnp.stack(v_s),
            jnp.stack(st_s))


def kernel(x_prompt, x_sample, cache_k, cache_v, state_gla, page_table, norm_sub, final_norm, w_ffn_gate, w_ffn_up, w_ffn_down, w_qkv_a, lambda_q1, lambda_k1, lambda_q2, lambda_k2, subln_a, w_o_a, w_proj_g, w_gk1, w_gk2, b_gk, norm_g, w_o_g):
    return _forward(x_prompt, x_sample, cache_k, cache_v, state_gla, page_table, norm_sub, final_norm,
                    w_ffn_gate, w_ffn_up, w_ffn_down, w_qkv_a, lambda_q1, lambda_k1, lambda_q2, lambda_k2,
                    subln_a, w_o_a, w_proj_g, w_gk1, w_gk2, b_gk, norm_g, w_o_g, CFG)
```

```python
import functools
import math

import jax
import jax.numpy as jnp
from jax import lax
from jax.experimental import pallas as pl
from jax.experimental.pallas import tpu as pltpu

F32 = jnp.float32
BF16 = jnp.bfloat16

RMS_EPS = 1e-6
ROPE_THETA = 500000.0
FFN_RESIDUAL_SCALE = 0.5
LOG2_E = 1.4426950408889634
LANES = 128
BF16_SUBLANES = 16
VMEM_LIMIT_BYTES = 56 * 1024 * 1024
GLA_INTRA_SPAN = 256

CFG = dict(
    attn_heads=8, gla_heads=4, rot_dim=32, gla_gate_norm=16.0, gla_chunk=64, n_mixers=2,
    ffn_tm=1024, ffn_tf=512, proj_tm=1024, proj_tn=1024, qkv_tm=1024, qkv_tn=512, out_tm=512,
    attn_tq=1024, attn_tk=512, decode_pages=8, gla_rows=512,
)


def _cparams(*sem):
    return pltpu.CompilerParams(dimension_semantics=sem, vmem_limit_bytes=VMEM_LIMIT_BYTES)


def _rms(x, gain):
    return x * lax.rsqrt(jnp.mean(x * x, axis=-1, keepdims=True) + RMS_EPS) * gain


def _silu(x):
    return x * jax.nn.sigmoid(x)


def _dot(a, b):
    return jnp.dot(a, b, preferred_element_type=F32)


def _dot_nt(a, b):
    return lax.dot_general(a, b, (((1,), (1,)), ((), ())), preferred_element_type=F32)


def _dot_tn(a, b):
    return lax.dot_general(a, b, (((0,), (0,)), ((), ())), preferred_element_type=F32)


def _layer_spec(block, index_map, lead):
    return pl.BlockSpec((None,) * len(lead) + block, lambda *g: tuple(lead) + tuple(index_map(*g)))


def _ffn_kernel(x_ref, g_ref, wg_ref, wu_ref, wd_ref, o_ref, h_ref):
    @pl.when(pl.program_id(1) == 0)
    def _():
        x = x_ref[...]
        h_ref[...] = _rms(x, g_ref[...]).astype(BF16)
        o_ref[...] = x

    h = h_ref[...]
    a = (_silu(_dot(h, wg_ref[...])) * (FFN_RESIDUAL_SCALE * _dot(h, wu_ref[...]))).astype(BF16)
    o_ref[...] += _dot(a, wd_ref[...])


def ffn_half(x, gain, wg, wu, wd, lead, *, tm, tf):
    m, d = x.shape
    ffp = wg.shape[-1]
    tm = min(tm, m)
    return pl.pallas_call(
        _ffn_kernel,
        grid=(m // tm, ffp // tf),
        in_specs=[
            pl.BlockSpec((tm, d), lambda i, f: (i, 0)),
            pl.BlockSpec((1, d), lambda i, f: (0, 0)),
            _layer_spec((d, tf), lambda i, f: (0, f), lead),
            _layer_spec((d, tf), lambda i, f: (0, f), lead),
            _layer_spec((tf, d), lambda i, f: (f, 0), lead),
        ],
        out_specs=pl.BlockSpec((tm, d), lambda i, f: (i, 0)),
        out_shape=jax.ShapeDtypeStruct((m, d), F32),
        scratch_shapes=[pltpu.VMEM((tm, d), BF16)],
        compiler_params=_cparams("parallel", "arbitrary"),
        name="ffn_half",
    )(x, gain.reshape(1, d), wg, wu, wd)


def _ffn_cast_kernel(x_ref, g_ref, wg_ref, wu_ref, wd_ref, o_ref, wgb_ref, wub_ref, wdb_ref, h_ref, *, ff):
    f = pl.program_id(0)
    tf = wg_ref.shape[1]
    valid = ff - f * tf
    col_ok = lax.broadcasted_iota(jnp.int32, (1, tf), 1) < valid
    row_ok = lax.broadcasted_iota(jnp.int32, (tf, 1), 0) < valid
    wg = jnp.where(col_ok, wg_ref[...], 0.0).astype(BF16)
    wu = jnp.where(col_ok, wu_ref[...], 0.0).astype(BF16)
    wd = jnp.where(row_ok, wd_ref[...], 0.0).astype(BF16)
    wgb_ref[...] = wg
    wub_ref[...] = wu
    wdb_ref[...] = wd

    @pl.when(f == 0)
    def _():
        x = x_ref[...]
        h_ref[...] = _rms(x, g_ref[...]).astype(BF16)
        o_ref[...] = x

    h = h_ref[...]
    a = (_silu(_dot(h, wg)) * (FFN_RESIDUAL_SCALE * _dot(h, wu))).astype(BF16)
    o_ref[...] += _dot(a, wd)


def ffn_half_cast(x, gain, wg, wu, wd, lead, *, tf):
    m, d = x.shape
    ff = wg.shape[-1]
    nf = pl.cdiv(ff, tf)
    ffp = nf * tf
    return pl.pallas_call(
        functools.partial(_ffn_cast_kernel, ff=ff),
        grid=(nf,),
        in_specs=[
            pl.BlockSpec((m, d), lambda f: (0, 0)),
            pl.BlockSpec((1, d), lambda f: (0, 0)),
            _layer_spec((d, tf), lambda f: (0, f), lead),
            _layer_spec((d, tf), lambda f: (0, f), lead),
            _layer_spec((tf, d), lambda f: (f, 0), lead),
        ],
        out_specs=[
            pl.BlockSpec((m, d), lambda f: (0, 0)),
            pl.BlockSpec((d, tf), lambda f: (0, f)),
            pl.BlockSpec((d, tf), lambda f: (0, f)),
            pl.BlockSpec((tf, d), lambda f: (f, 0)),
        ],
        out_shape=[jax.ShapeDtypeStruct((m, d), F32), jax.ShapeDtypeStruct((d, ffp), BF16),
                   jax.ShapeDtypeStruct((d, ffp), BF16), jax.ShapeDtypeStruct((ffp, d), BF16)],
        scratch_shapes=[pltpu.VMEM((m, d), BF16)],
        compiler_params=_cparams("arbitrary"),
        name="ffn_half_cast",
    )(x, gain.reshape(1, d), wg, wu, wd)


def _norm_matmul_kernel(x_ref, g_ref, w_ref, o_ref, h_ref):
    @pl.when(pl.program_id(1) == 0)
    def _():
        h_ref[...] = _rms(x_ref[...], g_ref[...]).astype(BF16)

    o_ref[...] = _dot(h_ref[...], w_ref[...])


def norm_matmul(x, gain, w, lead, *, tm, tn):
    m, d = x.shape
    n = w.shape[-1]
    tm = min(tm, m)
    return pl.pallas_call(
        _norm_matmul_kernel,
        grid=(m // tm, n // tn),
        in_specs=[
            pl.BlockSpec((tm, d), lambda i, j: (i, 0)),
            pl.BlockSpec((1, d), lambda i, j: (0, 0)),
            _layer_spec((d, tn), lambda i, j: (0, j), lead),
        ],
        out_specs=pl.BlockSpec((tm, tn), lambda i, j: (i, j)),
        out_shape=jax.ShapeDtypeStruct((m, n), F32),
        scratch_shapes=[pltpu.VMEM((tm, d), BF16)],
        compiler_params=_cparams("parallel", "arbitrary"),
        name="norm_matmul",
    )(x, gain.reshape(1, d), w)


def _matmul_residual_kernel(a_ref, w_ref, x_ref, o_ref):
    o_ref[...] = x_ref[...] + _dot(a_ref[...], w_ref[...])


def matmul_residual(a, w, x, lead, *, tm):
    m, k = a.shape
    n = w.shape[-1]
    tm = min(tm, m)
    return pl.pallas_call(
        _matmul_residual_kernel,
        grid=(m // tm,),
        in_specs=[
            pl.BlockSpec((tm, k), lambda i: (i, 0)),
            _layer_spec((k, n), lambda i: (0, 0), lead),
            pl.BlockSpec((tm, n), lambda i: (i, 0)),
        ],
        out_specs=pl.BlockSpec((tm, n), lambda i: (i, 0)),
        out_shape=jax.ShapeDtypeStruct((m, n), F32),
        compiler_params=_cparams("parallel"),
        name="matmul_residual",
    )(a, w, x)


def _final_norm_kernel(x_ref, g_ref, o_ref):
    o_ref[...] = _rms(x_ref[...], g_ref[...])


def final_norm(x, gain, *, tm):
    m, d = x.shape
    tm = min(tm, m)
    return pl.pallas_call(
        _final_norm_kernel,
        grid=(m // tm,),
        in_specs=[pl.BlockSpec((tm, d), lambda i: (i, 0)), pl.BlockSpec((1, d), lambda i: (0, 0))],
        out_specs=pl.BlockSpec((tm, d), lambda i: (i, 0)),
        out_shape=jax.ShapeDtypeStruct((m, d), F32),
        compiler_params=_cparams("parallel"),
        name="final_norm",
    )(x, gain.reshape(1, d))


def _rotary_tables(pos, rot_dim):
    inv_freq = jnp.power(ROPE_THETA, -jnp.arange(0, rot_dim, 2, dtype=F32) / rot_dim)
    ang = pos.astype(F32)[:, None] * inv_freq[None, :]
    cos, sin = jnp.cos(ang), jnp.sin(ang)
    n = pos.shape[0]
    cos_t = jnp.concatenate([cos, cos, jnp.ones((n, LANES - rot_dim), F32)], axis=-1)
    sin_t = jnp.concatenate([-sin, sin, jnp.zeros((n, LANES - rot_dim), F32)], axis=-1)
    return cos_t, sin_t


def _rotate(t, cos_t, sin_t, rot_dim):
    half = rot_dim // 2
    lane = lax.broadcasted_iota(jnp.int32, (t.shape[0], LANES), 1)
    segs = []
    for s in range(t.shape[1] // LANES):
        xs = t[:, s * LANES:(s + 1) * LANES]
        up = pltpu.roll(xs, LANES - half, 1)
        down = pltpu.roll(xs, half, 1)
        partner = jnp.where(lane < half, up, down)
        segs.append(jnp.where(lane < rot_dim, xs * cos_t + partner * sin_t, xs))
    return jnp.concatenate(segs, axis=1)


def _qkv_kernel(x_ref, g_ref, wq_ref, wk_ref, wv_ref, cos_ref, sin_ref, *refs, rot_dim):
    qb_ref, kf_ref, vf_ref, h_ref = refs[-4:]

    @pl.when(pl.program_id(1) == 0)
    def _():
        h_ref[...] = _rms(x_ref[...], g_ref[...]).astype(BF16)

    h = h_ref[...]
    cos_t, sin_t = cos_ref[...], sin_ref[...]
    qb_ref[...] = _rotate(_dot(h, wq_ref[...]), cos_t, sin_t, rot_dim).astype(BF16)
    kf_ref[...] = _rotate(_dot(h, wk_ref[...]), cos_t, sin_t, rot_dim)
    vf_ref[...] = _dot(h, wv_ref[...])


def qkv_project(x, gain, w_qkv, layer, kv_stacks, cos_t, sin_t, *, tm, tn, rot_dim):
    m, d = x.shape
    n_layers = w_qkv.shape[0]
    da = w_qkv.shape[-1] // 3
    tm = min(tm, m)
    nj = da // tn
    ntab = cos_t.shape[0] // tm
    wspec = lambda sec: _layer_spec((d, tn), lambda i, j: (0, sec * nj + j), (layer,))
    tspec = pl.BlockSpec((tm, LANES), lambda i, j: (i % ntab, 0))
    qspec = pl.BlockSpec((tm, tn), lambda i, j: (i, j))
    kvspec = _layer_spec((tm, tn), lambda i, j: (i, j), (layer,))
    stacks = () if kv_stacks is None else tuple(kv_stacks)
    n_in = 7
    return pl.pallas_call(
        functools.partial(_qkv_kernel, rot_dim=rot_dim),
        grid=(m // tm, nj),
        in_specs=[
            pl.BlockSpec((tm, d), lambda i, j: (i, 0)),
            pl.BlockSpec((1, d), lambda i, j: (0, 0)),
            wspec(0), wspec(1), wspec(2), tspec, tspec,
        ] + [pl.BlockSpec(memory_space=pl.ANY)] * len(stacks),
        out_specs=[qspec, kvspec, kvspec],
        out_shape=[jax.ShapeDtypeStruct((m, da), BF16)] + [jax.ShapeDtypeStruct((n_layers, m, da), F32)] * 2,
        input_output_aliases={n_in + t: 1 + t for t in range(len(stacks))},
        scratch_shapes=[pltpu.VMEM((tm, d), BF16)],
        compiler_params=_cparams("parallel", "arbitrary"),
        name="qkv_project",
    )(x, gain.reshape(1, d), w_qkv, w_qkv, w_qkv, cos_t, sin_t, *stacks)


def _lambda(lq1_ref, lk1_ref, lq2_ref, lk2_ref, lam_init):
    e1 = jnp.exp(jnp.sum(lq1_ref[...] * lk1_ref[...], axis=-1, keepdims=True))
    e2 = jnp.exp(jnp.sum(lq2_ref[...] * lk2_ref[...], axis=-1, keepdims=True))
    return e1 - e2 + lam_init


def _attn_kernel(q_ref, k_ref, v_ref, lq1_ref, lk1_ref, lq2_ref, lk2_ref, sub_ref,
                 o_ref, kb_ref, vb_ref, m_ref, l_ref, acc_ref, *, tq, tk, hd, lam_init):
    i = pl.program_id(2)
    c_scale = (hd ** -0.5) * LOG2_E
    vd = 2 * hd

    @pl.when(i == 0)
    def _():
        kb_ref[...] = k_ref[...].astype(BF16)
        vb_ref[...] = v_ref[...].astype(BF16)

    m_ref[...] = jnp.full_like(m_ref, -jnp.inf)
    l_ref[...] = jnp.zeros_like(l_ref)
    acc_ref[...] = jnp.zeros_like(acc_ref)
    q = q_ref[...]

    def kv_block(j, masked):
        start = pl.multiple_of(j * tk, tk)
        k = kb_ref[pl.ds(start, tk), :]
        v = vb_ref[pl.ds(start, tk), :]
        if masked:
            row = lax.broadcasted_iota(jnp.int32, (tq, tk), 0) + i * tq
            col = lax.broadcasted_iota(jnp.int32, (tq, tk), 1) + j * tk
            visible = col <= row
        for c in range(2):
            sc = _dot_nt(q[:, c * hd:(c + 1) * hd], k[:, c * hd:(c + 1) * hd]) * c_scale
            if masked:
                sc = jnp.where(visible, sc, -jnp.inf)
            m_old = m_ref[c]
            m_new = jnp.maximum(m_old, jnp.max(sc, axis=-1, keepdims=True))
            alpha = jnp.exp2(m_old - m_new)
            p = jnp.exp2(sc - jnp.tile(m_new, (1, tk // LANES)))
            l_ref[c] = alpha * l_ref[c] + jnp.sum(p, axis=-1, keepdims=True)
            acc_ref[c] = jnp.tile(alpha, (1, vd // LANES)) * acc_ref[c] + _dot(p.astype(BF16), v)
            m_ref[c] = m_new

    n_below = (i * tq) // tk

    def body(j, carry):
        kv_block(j, False)
        return carry

    lax.fori_loop(0, n_below, body, 0)
    for dj in range(tq // tk):
        kv_block(n_below + dj, True)

    lam = _lambda(lq1_ref, lk1_ref, lq2_ref, lk2_ref, lam_init)
    l0 = jnp.tile(l_ref[0], (1, vd // LANES))
    l1 = jnp.tile(l_ref[1], (1, vd // LANES))
    o = acc_ref[0] / l0 - lam * (acc_ref[1] / l1)
    o_ref[...] = (_rms(o, sub_ref[...]) * (1.0 - lam_init)).astype(BF16)


def diff_attention_prompt(q, k, v, layer, lam_params, subln, *, batch, heads, tq, tk, lam_init):
    m, da = q.shape
    vd = da // heads
    hd = vd // 2
    seq = m // batch
    tq, tk = min(tq, seq), min(tk, seq)
    assert tq % tk == 0 and seq % tq == 0
    nq = seq // tq
    qspec = pl.BlockSpec((tq, vd), lambda b, h, i: (b * nq + i, h))
    kspec = _layer_spec((seq, vd), lambda b, h, i: (b, h), (layer,))
    pspec = pl.BlockSpec((1, hd), lambda b, h, i: (0, 0))
    return pl.pallas_call(
        functools.partial(_attn_kernel, tq=tq, tk=tk, hd=hd, lam_init=lam_init),
        grid=(batch, heads, nq),
        in_specs=[qspec, kspec, kspec, pspec, pspec, pspec, pspec, pl.BlockSpec((1, vd), lambda b, h, i: (0, 0))],
        out_specs=qspec,
        out_shape=jax.ShapeDtypeStruct((m, da), BF16),
        scratch_shapes=[pltpu.VMEM((seq, vd), BF16), pltpu.VMEM((seq, vd), BF16),
                        pltpu.VMEM((2, tq, LANES), F32), pltpu.VMEM((2, tq, LANES), F32),
                        pltpu.VMEM((2, tq, vd), F32)],
        compiler_params=_cparams("parallel", "parallel", "arbitrary"),
        name="diff_attention_prompt",
    )(q, k, v, *[p.reshape(1, hd) for p in lam_params], subln.reshape(1, vd))


def _decode_kernel(pt_ref, q_ref, *refs, heads, hd, lam_init, pps):
    kc_refs, vc_refs = refs[:pps], refs[pps:2 * pps]
    (kn_ref, vn_ref, lq1_ref, lk1_ref, lq2_ref, lk2_ref, sub_ref, o_ref, m_ref, l_ref, acc_ref) = refs[2 * pps:]
    p = pl.program_id(1)
    rows = 2 * heads
    c_scale = (hd ** -0.5) * LOG2_E

    @pl.when(p == 0)
    def _():
        m_ref[...] = jnp.full_like(m_ref, -jnp.inf)
        l_ref[...] = jnp.zeros_like(l_ref)
        acc_ref[...] = jnp.zeros_like(acc_ref)

    q = q_ref[...]
    lane = lax.broadcasted_iota(jnp.int32, q.shape, 1)
    qmat = jnp.concatenate([jnp.where(lane < hd, q, 0.0), jnp.where(lane >= hd, q, 0.0)], axis=0).astype(BF16)

    def online_update(scs, pv_of):
        m_old = m_ref[...]
        m_new = m_old
        for sc in scs:
            m_new = jnp.maximum(m_new, jnp.max(sc, axis=-1, keepdims=True))
        alpha = jnp.exp2(m_old - m_new)
        l_new = alpha * l_ref[...]
        acc = alpha * acc_ref[...]
        for r, sc in enumerate(scs):
            pr = jnp.exp2(sc - m_new)
            l_new = l_new + jnp.sum(pr, axis=-1, keepdims=True)
            acc = acc + pv_of(r, pr.astype(BF16))
        l_ref[...] = l_new
        acc_ref[...] = acc
        m_ref[...] = m_new

    n = kc_refs[0].shape[0]
    rh = lax.broadcasted_iota(jnp.int32, (rows, n), 0) & (heads - 1)
    ch = lax.broadcasted_iota(jnp.int32, (rows, n), 1) & (heads - 1)
    same_head = rh == ch
    scs = [jnp.where(same_head, _dot_nt(qmat, kc[...].astype(BF16)) * c_scale, -jnp.inf) for kc in kc_refs]
    online_update(scs, lambda r, pr: _dot(pr, vc_refs[r][...].astype(BF16)))

    @pl.when(p == pl.num_programs(1) - 1)
    def _():
        kn = kn_ref[...].astype(BF16).astype(F32)
        vn = vn_ref[...].astype(BF16).astype(F32)
        kn2 = jnp.concatenate([kn, kn], axis=0)
        vn2 = jnp.concatenate([vn, vn], axis=0)
        sc_new = jnp.sum(qmat.astype(F32) * kn2, axis=-1, keepdims=True) * c_scale
        online_update([sc_new], lambda r, pr: pr.astype(F32) * vn2)
        lam = _lambda(lq1_ref, lk1_ref, lq2_ref, lk2_ref, lam_init)
        on = acc_ref[...] / l_ref[...]
        o = on[:heads] - lam * on[heads:]
        o_ref[...] = _rms(o, sub_ref[...]) * (1.0 - lam_init)


def diff_attention_decode(q, k_new, v_new, cache_k, cache_v, layer, page_table, lam_params, subln,
                          *, heads, lam_init, pages_per_step):
    bs, _, vd = q.shape
    hd = vd // 2
    assert heads & (heads - 1) == 0, "same-head mask uses a power-of-two head count"
    n_layers, n_phys, page, _, _ = cache_k.shape
    n_pages = page_table.shape[1]
    pps = math.gcd(n_pages, pages_per_step)
    ck = cache_k.reshape(n_layers, n_phys, page * heads, vd)
    cv = cache_v.reshape(n_layers, n_phys, page * heads, vd)
    tok = pl.BlockSpec((None, heads, vd), lambda b, p, pt: (b, 0, 0))
    cspec = lambda r: pl.BlockSpec((None, None, page * heads, vd),
                                   lambda b, p, pt: (layer, pt[b * n_pages + p * pps + r], 0, 0))
    pspec = pl.BlockSpec((1, hd), lambda b, p, pt: (0, 0))
    return pl.pallas_call(
        functools.partial(_decode_kernel, heads=heads, hd=hd, lam_init=lam_init, pps=pps),
        grid_spec=pltpu.PrefetchScalarGridSpec(
            num_scalar_prefetch=1,
            grid=(bs, n_pages // pps),
            in_specs=[tok] + [cspec(r) for r in range(pps)] * 2 + [tok, tok, pspec, pspec, pspec, pspec,
                      pl.BlockSpec((1, vd), lambda b, p, pt: (0, 0))],
            out_specs=tok,
            scratch_shapes=[pltpu.VMEM((2 * heads, 1), F32), pltpu.VMEM((2 * heads, 1), F32),
                            pltpu.VMEM((2 * heads, vd), F32)],
        ),
        out_shape=jax.ShapeDtypeStruct((bs, heads, vd), F32),
        compiler_params=_cparams("parallel", "arbitrary"),
        name="diff_attention_decode",
    )(page_table.reshape(-1), q, *([ck] * pps), *([cv] * pps), k_new, v_new,
      *[p.reshape(1, hd) for p in lam_params], subln.reshape(1, vd))


def _log_sigmoid(x):
    return jnp.minimum(x, 0.0) - jnp.log1p(jnp.exp(-jnp.abs(x)))


def _gla_gate_kernel(x_ref, g_ref, w1_ref, w2_ref, b_ref, o_ref, *, gate_norm):
    h = _rms(x_ref[...], g_ref[...]).astype(BF16)
    low = _dot(h, w1_ref[...]).astype(BF16)
    gk = _dot(low, w2_ref[...]) + b_ref[...]
    o_ref[...] = _log_sigmoid(gk) / gate_norm


def gla_gate(x, gain, w1, w2, bias, *, tm, gate_norm):
    m, d = x.shape
    r = w1.shape[1]
    n = w2.shape[1]
    tm = min(tm, m)
    full = lambda shape: pl.BlockSpec(shape, lambda i: (0, 0))
    return pl.pallas_call(
        functools.partial(_gla_gate_kernel, gate_norm=gate_norm),
        grid=(m // tm,),
        in_specs=[pl.BlockSpec((tm, d), lambda i: (i, 0)), full((1, d)), full((d, r)), full((r, n)), full((1, n))],
        out_specs=pl.BlockSpec((tm, n), lambda i: (i, 0)),
        out_shape=jax.ShapeDtypeStruct((m, n), F32),
        compiler_params=_cparams("parallel"),
        name="gla_gate",
    )(x, gain.reshape(1, d), w1, w2, bias.reshape(1, n))


def _split3(x):
    hi = x.astype(BF16)
    r1 = x - hi.astype(F32)
    mid = r1.astype(BF16)
    lo = (r1 - mid.astype(F32)).astype(BF16)
    return hi, mid, lo


def _gla_chunk_kernel(q_ref, k_ref, v_ref, og_ref, la_ref, ng_ref, y_ref, s_ref, st_ref, *, chunk, heads, span):
    r = pl.program_id(1)

    @pl.when(r == 0)
    def _():
        st_ref[...] = jnp.zeros_like(st_ref)

    rows = q_ref.shape[0]
    dk = q_ref.shape[1] // heads
    dv = v_ref.shape[1] // heads
    shift = chunk.bit_length() - 1
    ti = lax.broadcasted_iota(jnp.int32, (chunk, chunk), 0)
    si = lax.broadcasted_iota(jnp.int32, (chunk, chunk), 1)
    tri = (si <= ti).astype(BF16)
    tb = lax.broadcasted_iota(jnp.int32, (span, span), 0)
    sb = lax.broadcasted_iota(jnp.int32, (span, span), 1)
    causal = ((tb >> shift) == (sb >> shift)) & (sb <= tb)

    bs, bls = [], []
    for c in range(rows // chunk):
        hi, mid, lo = _split3(la_ref[c * chunk:(c + 1) * chunk, :])
        b_c = _dot(tri, hi) + _dot(tri, mid) + _dot(tri, lo)
        bs.append(b_c)
        bls.append(jnp.broadcast_to(b_c[chunk - 1:chunk, :], b_c.shape))
    b = jnp.concatenate(bs, axis=0)
    b_last = jnp.concatenate(bls, axis=0)
    k = k_ref[...]
    q_e = (q_ref[...] * (dk ** -0.5) * jnp.exp(b)).astype(BF16)
    k_e = (k * jnp.exp(-b)).astype(BF16)
    k_tail = (k * jnp.exp(b_last - b)).astype(BF16)
    decay = jnp.exp(b_last)
    v = v_ref[...].astype(BF16)
    ng = ng_ref[...]
    for h in range(heads):
        ks = slice(h * dk, (h + 1) * dk)
        vs = slice(h * dv, (h + 1) * dv)
        intra = []
        for u in range(rows // span):
            us = slice(u * span, (u + 1) * span)
            attn = jnp.where(causal, _dot_nt(q_e[us, ks], k_e[us, ks]), 0.0).astype(BF16)
            intra.append(_dot(attn, v[us, vs]))
        st = st_ref[h]
        inter = []
        for c in range(rows // chunk):
            sl = slice(c * chunk, (c + 1) * chunk)
            inter.append(_dot_nt(q_e[sl, ks], st.astype(BF16)))
            st = decay[c * chunk:c * chunk + 1, ks] * st + _dot_tn(v[sl, vs], k_tail[sl, ks])
        st_ref[h] = st
        o = jnp.concatenate(intra, axis=0) + jnp.concatenate(inter, axis=0)
        y_ref[:, vs] = (_rms(o, ng) * _silu(og_ref[:, vs])).astype(BF16)

    @pl.when(r == pl.num_programs(1) - 1)
    def _():
        for h in range(heads):
            s_ref[h] = st_ref[h].T


def gla_prompt_core(proj, log_a, norm_g, *, batch, heads, rows, chunk):
    m = proj.shape[0]
    kd = log_a.shape[1]
    vd = (proj.shape[1] - 2 * kd) // 2
    dk, dv = kd // heads, vd // heads
    seq = m // batch
    rows = min(rows, seq)
    assert chunk & (chunk - 1) == 0 and rows % chunk == 0, "chunk ids come from a shift"
    nr = seq // rows
    rowblk = lambda col0, width: pl.BlockSpec((rows, width), lambda b, r: (b * nr + r, col0 // width))
    return pl.pallas_call(
        functools.partial(_gla_chunk_kernel, chunk=chunk, heads=heads, span=min(rows, GLA_INTRA_SPAN)),
        grid=(batch, nr),
        in_specs=[rowblk(0, kd), rowblk(kd, kd), rowblk(2 * kd, vd), rowblk(2 * kd + vd, vd),
                  pl.BlockSpec((rows, kd), lambda b, r: (b * nr + r, 0)),
                  pl.BlockSpec((1, dv), lambda b, r: (0, 0))],
        out_specs=[pl.BlockSpec((rows, vd), lambda b, r: (b * nr + r, 0)),
                   pl.BlockSpec((None, heads, dk, dv), lambda b, r: (b, 0, 0, 0))],
        out_shape=[jax.ShapeDtypeStruct((m, vd), BF16), jax.ShapeDtypeStruct((batch, heads, dk, dv), F32)],
        scratch_shapes=[pltpu.VMEM((heads, dv, dk), F32)],
        compiler_params=_cparams("parallel", "arbitrary"),
        name="gla_prompt_core",
    )(proj, proj, proj, proj, log_a, norm_g.reshape(1, dv))


def _gla_step_kernel(q_ref, kcol_ref, gcol_ref, v_ref, og_ref, s0_ref, ng_ref, y_ref, s_ref):
    heads, dk, _ = s0_ref.shape
    for h in range(heads):
        s_new = jnp.exp(gcol_ref[h]) * s0_ref[h] + kcol_ref[h] * v_ref[h]
        s_ref[h] = s_new
        q = jnp.broadcast_to(q_ref[h] * (dk ** -0.5), (8, dk)).astype(BF16)
        o = _dot(q, s_new.astype(BF16))[0:1, :]
        y_ref[h] = _rms(o, ng_ref[...]) * _silu(og_ref[h])


def gla_decode_step(q, k, log_a, v, og, state, norm_g):
    bs, heads, dk = q.shape
    dv = v.shape[-1]
    row = lambda w: pl.BlockSpec((None, heads, 1, w), lambda b: (b, 0, 0, 0))
    col = pl.BlockSpec((None, heads, dk, 1), lambda b: (b, 0, 0, 0))
    mat = pl.BlockSpec((None, heads, dk, dv), lambda b: (b, 0, 0, 0))
    y, s = pl.pallas_call(
        _gla_step_kernel,
        grid=(bs,),
        in_specs=[row(dk), col, col, row(dv), row(dv), mat, pl.BlockSpec((1, dv), lambda b: (0, 0))],
        out_specs=[row(dv), mat],
        out_shape=[jax.ShapeDtypeStruct((bs, heads, 1, dv), F32), jax.ShapeDtypeStruct((bs, heads, dk, dv), F32)],
        compiler_params=_cparams("parallel"),
        name="gla_decode_step",
    )(q[:, :, None, :], k[..., None], log_a[..., None], v[:, :, None, :], og[:, :, None, :], state,
      norm_g.reshape(1, dv))
    return y.reshape(bs, heads * dv), s


def _pad_axis(a, axis, mult):
    pad = (-a.shape[axis]) % mult
    if pad == 0:
        return a
    widths = [(0, 0)] * a.ndim
    widths[axis] = (0, pad)
    return jnp.pad(a, widths)


def _forward(x_prompt, x_sample, cache_k, cache_v, state_gla, page_table, norm_sub, final_norm_g,
             w_ffn_gate, w_ffn_up, w_ffn_down, w_qkv_a, lambda_q1, lambda_k1, lambda_q2, lambda_k2,
             subln_a, w_o_a, w_proj_g, w_gk1, w_gk2, b_gk, norm_g, w_o_g, cfg):
    batch, seq, d = x_prompt.shape
    bs = x_sample.shape[0]
    depth = norm_sub.shape[0]
    heads, gheads = cfg["attn_heads"], cfg["gla_heads"]
    n_past = page_table.shape[1] * cache_k.shape[2]
    da = w_qkv_a.shape[2] // 3
    vd_attn = da // heads
    kd = w_gk2.shape[2]
    vd = (w_proj_g.shape[2] - 2 * kd) // 2
    dk, dv = kd // gheads, vd // gheads
    bs_pad = -(-bs // BF16_SUBLANES) * BF16_SUBLANES

    xp = x_prompt.reshape(batch * seq, d)
    xs = _pad_axis(x_sample.reshape(bs, d), 0, BF16_SUBLANES)

    tf = cfg["ffn_tf"]
    wqkv = w_qkv_a.astype(BF16)
    woa = w_o_a.astype(BF16)
    wproj = w_proj_g.astype(BF16)
    wog = w_o_g.astype(BF16)
    wgk1 = _pad_axis(w_gk1.astype(BF16), 2, LANES)
    wgk2 = _pad_axis(w_gk2.astype(BF16), 1, LANES)

    cos_p, sin_p = _rotary_tables(jnp.arange(seq), cfg["rot_dim"])
    cos_s, sin_s = _rotary_tables(jnp.full((bs_pad,), n_past, jnp.int32), cfg["rot_dim"])

    def ffn(xp, xs, i, half):
        gain = norm_sub[i, 2 * half]
        xs, wg, wu, wd = ffn_half_cast(xs, gain, w_ffn_gate, w_ffn_up, w_ffn_down, (i, half), tf=tf)
        return ffn_half(xp, gain, wg, wu, wd, (), tm=cfg["ffn_tm"], tf=tf), xs

    proj_t = dict(tm=cfg["proj_tm"], tn=cfg["proj_tn"])
    qkv_t = dict(tm=cfg["qkv_tm"], tn=cfg["qkv_tn"], rot_dim=cfg["rot_dim"])

    kv_p, kv_s, st_p, st_s = None, None, [], []
    for i in range(depth):
        j = i // cfg["n_mixers"]
        xp, xs = ffn(xp, xs, i, 0)
        if i % cfg["n_mixers"] == 0:
            lam_init = 0.8 - 0.6 * math.exp(-0.3 * i)
            lam_params = (lambda_q1[j], lambda_k1[j], lambda_q2[j], lambda_k2[j])
            qb, *kv_p = qkv_project(xp, norm_sub[i, 1], wqkv, j, kv_p, cos_p, sin_p, **qkv_t)
            a = diff_attention_prompt(qb, *kv_p, j, lam_params, subln_a[j], batch=batch, heads=heads,
                                      tq=cfg["attn_tq"], tk=cfg["attn_tk"], lam_init=lam_init)
            xp = matmul_residual(a, woa, xp, (j,), tm=cfg["out_tm"])

            qb, *kv_s = qkv_project(xs, norm_sub[i, 1], wqkv, j, kv_s, cos_s, sin_s, **qkv_t)
            tok = lambda t: t[:bs].reshape(bs, heads, vd_attn)
            o = diff_attention_decode(tok(qb).astype(F32), tok(kv_s[0][j]), tok(kv_s[1][j]), cache_k, cache_v, j,
                                      page_table, lam_params, subln_a[j], heads=heads, lam_init=lam_init,
                                      pages_per_step=cfg["decode_pages"])
            a = _pad_axis(o.reshape(bs, da), 0, BF16_SUBLANES).astype(BF16)
            xs = matmul_residual(a, woa, xs, (j,), tm=cfg["out_tm"])
        else:
            gate = functools.partial(gla_gate, tm=cfg["proj_tm"], gate_norm=cfg["gla_gate_norm"])
            proj = norm_matmul(xp, norm_sub[i, 1], wproj, (j,), **proj_t)
            log_a = gate(xp, norm_sub[i, 1], wgk1[j], wgk2[j], b_gk[j])
            y, s_fin = gla_prompt_core(proj, log_a, norm_g[j], batch=batch, heads=gheads,
                                       rows=cfg["gla_rows"], chunk=cfg["gla_chunk"])
            st_p.append(s_fin)
            xp = matmul_residual(y, wog, xp, (j,), tm=cfg["out_tm"])

            proj = norm_matmul(xs, norm_sub[i, 1], wproj, (j,), **proj_t)[:bs]
            log_a = gate(xs, norm_sub[i, 1], wgk1[j], wgk2[j], b_gk[j])[:bs]
            y, s_new = gla_decode_step(
                proj[:, :kd].reshape(bs, gheads, dk), proj[:, kd:2 * kd].reshape(bs, gheads, dk),
                log_a.reshape(bs, gheads, dk), proj[:, 2 * kd:2 * kd + vd].reshape(bs, gheads, dv),
                proj[:, 2 * kd + vd:].reshape(bs, gheads, dv), state_gla[j], norm_g[j])
            st_s.append(s_new)
            xs = matmul_residual(_pad_axis(y, 0, BF16_SUBLANES).astype(BF16), wog, xs, (j,), tm=cfg["out_tm"])
        xp, xs = ffn(xp, xs, i, 1)

    yp = final_norm(xp, final_norm_g, tm=cfg["out_tm"]).reshape(batch, seq, d)
    ys = final_norm(xs, final_norm_g, tm=cfg["out_tm"])[:bs].reshape(bs, 1, d)
    n_attn = w_qkv_a.shape[0]
    k_p, v_p = (t.reshape(n_attn, batch, seq, heads, vd_attn) for t in kv_p)
    k_s, v_s = (t[:, :bs].reshape(n_attn, bs, 1, heads, vd_attn) for t in kv_s)
    return yp, ys, k_p, v_p, jnp.stack(st_p), k_s, v_s, jnp.stack(st_s)


def kernel(x_prompt, x_sample, cache_k, cache_v, state_gla, page_table, norm_sub, final_norm, w_ffn_gate, w_ffn_up, w_ffn_down, w_qkv_a, lambda_q1, lambda_k1, lambda_q2, lambda_k2, subln_a, w_o_a, w_proj_g, w_gk1, w_gk2, b_gk, norm_g, w_o_g):
    return _forward(x_prompt, x_sample, cache_k, cache_v, state_gla, page_table, norm_sub, final_norm,
                    w_ffn_gate, w_ffn_up, w_ffn_down, w_qkv_a, lambda_q1, lambda_k1, lambda_q2, lambda_k2,
                    subln_a, w_o_a, w_proj_g, w_gk1, w_gk2, b_gk, norm_g, w_o_g, CFG)
```

```python
import functools
import math

import jax
import jax.numpy as jnp
from jax import lax
from jax.experimental import pallas as pl
from jax.experimental.pallas import tpu as pltpu

F32 = jnp.float32
BF16 = jnp.bfloat16

RMS_EPS = 1e-6
ROPE_THETA = 500000.0
FFN_RESIDUAL_SCALE = 0.5
LOG2_E = 1.4426950408889634
LANES = 128
BF16_SUBLANES = 16
VMEM_LIMIT_BYTES = 56 * 1024 * 1024
GLA_INTRA_SPAN = 256

CFG = dict(
    attn_heads=8, gla_heads=4, rot_dim=32, gla_gate_norm=16.0, gla_chunk=64, n_mixers=2,
    ffn_tm=1024, ffn_tf=512, proj_tm=1024, proj_tn=1024, qkv_tm=1024, qkv_tn=512, out_tm=512,
    attn_tq=1024, attn_tk=512, attn_td=256, decode_pages=8, gla_rows=512,
)


def _cparams(*sem):
    return pltpu.CompilerParams(dimension_semantics=sem, vmem_limit_bytes=VMEM_LIMIT_BYTES)


def _rms(x, gain):
    return x * lax.rsqrt(jnp.mean(x * x, axis=-1, keepdims=True) + RMS_EPS) * gain


def _silu(x):
    return x * jax.nn.sigmoid(x)


def _dot(a, b):
    return jnp.dot(a, b, preferred_element_type=F32)


def _dot_nt(a, b):
    return lax.dot_general(a, b, (((1,), (1,)), ((), ())), preferred_element_type=F32)


def _dot_tn(a, b):
    return lax.dot_general(a, b, (((0,), (0,)), ((), ())), preferred_element_type=F32)


def _layer_spec(block, index_map, lead):
    return pl.BlockSpec((None,) * len(lead) + block, lambda *g: tuple(lead) + tuple(index_map(*g)))


def _ffn_kernel(x_ref, g_ref, wg_ref, wu_ref, wd_ref, *refs):
    o_ref, h_ref = refs[-2:]

    @pl.when(pl.program_id(1) == 0)
    def _():
        x = x_ref[...]
        h_ref[...] = _rms(x, g_ref[...]).astype(BF16)
        o_ref[...] = x

    h = h_ref[...]
    a = (_silu(_dot(h, wg_ref[...])) * (FFN_RESIDUAL_SCALE * _dot(h, wu_ref[...]))).astype(BF16)
    o_ref[...] += _dot(a, wd_ref[...])

    if len(refs) == 3:
        @pl.when(pl.program_id(1) == pl.num_programs(1) - 1)
        def _():
            o_ref[...] = _rms(o_ref[...], refs[0][...])


def ffn_half(x, gain, wg, wu, wd, lead, *, tm, tf, out_gain=None):
    m, d = x.shape
    ffp = wg.shape[-1]
    tm = min(tm, m)
    vec = pl.BlockSpec((1, d), lambda i, f: (0, 0))
    extra = [] if out_gain is None else [out_gain.reshape(1, d)]
    return pl.pallas_call(
        _ffn_kernel,
        grid=(m // tm, ffp // tf),
        in_specs=[
            pl.BlockSpec((tm, d), lambda i, f: (i, 0)),
            vec,
            _layer_spec((d, tf), lambda i, f: (0, f), lead),
            _layer_spec((d, tf), lambda i, f: (0, f), lead),
            _layer_spec((tf, d), lambda i, f: (f, 0), lead),
        ] + [vec] * len(extra),
        out_specs=pl.BlockSpec((tm, d), lambda i, f: (i, 0)),
        out_shape=jax.ShapeDtypeStruct((m, d), F32),
        scratch_shapes=[pltpu.VMEM((tm, d), BF16)],
        compiler_params=_cparams("parallel", "arbitrary"),
        name="ffn_half",
    )(x, gain.reshape(1, d), wg, wu, wd, *extra)


def _ffn_cast_kernel(x_ref, g_ref, wg_ref, wu_ref, wd_ref, o_ref, wgb_ref, wub_ref, wdb_ref, h_ref, *, ff):
    f = pl.program_id(0)
    tf = wg_ref.shape[1]
    valid = ff - f * tf
    col_ok = lax.broadcasted_iota(jnp.int32, (1, tf), 1) < valid
    row_ok = lax.broadcasted_iota(jnp.int32, (tf, 1), 0) < valid
    wg = jnp.where(col_ok, wg_ref[...], 0.0).astype(BF16)
    wu = jnp.where(col_ok, wu_ref[...], 0.0).astype(BF16)
    wd = jnp.where(row_ok, wd_ref[...], 0.0).astype(BF16)
    wgb_ref[...] = wg
    wub_ref[...] = wu
    wdb_ref[...] = wd

    @pl.when(f == 0)
    def _():
        x = x_ref[...]
        h_ref[...] = _rms(x, g_ref[...]).astype(BF16)
        o_ref[...] = x

    h = h_ref[...]
    a = (_silu(_dot(h, wg)) * (FFN_RESIDUAL_SCALE * _dot(h, wu))).astype(BF16)
    o_ref[...] += _dot(a, wd)


def ffn_half_cast(x, gain, wg, wu, wd, lead, *, tf):
    m, d = x.shape
    ff = wg.shape[-1]
    nf = pl.cdiv(ff, tf)
    ffp = nf * tf
    return pl.pallas_call(
        functools.partial(_ffn_cast_kernel, ff=ff),
        grid=(nf,),
        in_specs=[
            pl.BlockSpec((m, d), lambda f: (0, 0)),
            pl.BlockSpec((1, d), lambda f: (0, 0)),
            _layer_spec((d, tf), lambda f: (0, f), lead),
            _layer_spec((d, tf), lambda f: (0, f), lead),
            _layer_spec((tf, d), lambda f: (f, 0), lead),
        ],
        out_specs=[
            pl.BlockSpec((m, d), lambda f: (0, 0)),
            pl.BlockSpec((d, tf), lambda f: (0, f)),
            pl.BlockSpec((d, tf), lambda f: (0, f)),
            pl.BlockSpec((tf, d), lambda f: (f, 0)),
        ],
        out_shape=[jax.ShapeDtypeStruct((m, d), F32), jax.ShapeDtypeStruct((d, ffp), BF16),
                   jax.ShapeDtypeStruct((d, ffp), BF16), jax.ShapeDtypeStruct((ffp, d), BF16)],
        scratch_shapes=[pltpu.VMEM((m, d), BF16)],
        compiler_params=_cparams("arbitrary"),
        name="ffn_half_cast",
    )(x, gain.reshape(1, d), wg, wu, wd)


def _norm_matmul_kernel(x_ref, g_ref, w_ref, o_ref, h_ref):
    @pl.when(pl.program_id(1) == 0)
    def _():
        h_ref[...] = _rms(x_ref[...], g_ref[...]).astype(BF16)

    o_ref[...] = _dot(h_ref[...], w_ref[...])


def norm_matmul(x, gain, w, lead, *, tm, tn):
    m, d = x.shape
    n = w.shape[-1]
    tm = min(tm, m)
    return pl.pallas_call(
        _norm_matmul_kernel,
        grid=(m // tm, n // tn),
        in_specs=[
            pl.BlockSpec((tm, d), lambda i, j: (i, 0)),
            pl.BlockSpec((1, d), lambda i, j: (0, 0)),
            _layer_spec((d, tn), lambda i, j: (0, j), lead),
        ],
        out_specs=pl.BlockSpec((tm, tn), lambda i, j: (i, j)),
        out_shape=jax.ShapeDtypeStruct((m, n), F32),
        scratch_shapes=[pltpu.VMEM((tm, d), BF16)],
        compiler_params=_cparams("parallel", "arbitrary"),
        name="norm_matmul",
    )(x, gain.reshape(1, d), w)


def _matmul_residual_kernel(a_ref, w_ref, x_ref, o_ref):
    o_ref[...] = x_ref[...] + _dot(a_ref[...], w_ref[...])


def matmul_residual(a, w, x, lead, *, tm):
    m, k = a.shape
    n = w.shape[-1]
    tm = min(tm, m)
    return pl.pallas_call(
        _matmul_residual_kernel,
        grid=(m // tm,),
        in_specs=[
            pl.BlockSpec((tm, k), lambda i: (i, 0)),
            _layer_spec((k, n), lambda i: (0, 0), lead),
            pl.BlockSpec((tm, n), lambda i: (i, 0)),
        ],
        out_specs=pl.BlockSpec((tm, n), lambda i: (i, 0)),
        out_shape=jax.ShapeDtypeStruct((m, n), F32),
        compiler_params=_cparams("parallel"),
        name="matmul_residual",
    )(a, w, x)


def _final_norm_kernel(x_ref, g_ref, o_ref):
    o_ref[...] = _rms(x_ref[...], g_ref[...])


def final_norm(x, gain, *, tm):
    m, d = x.shape
    tm = min(tm, m)
    return pl.pallas_call(
        _final_norm_kernel,
        grid=(m // tm,),
        in_specs=[pl.BlockSpec((tm, d), lambda i: (i, 0)), pl.BlockSpec((1, d), lambda i: (0, 0))],
        out_specs=pl.BlockSpec((tm, d), lambda i: (i, 0)),
        out_shape=jax.ShapeDtypeStruct((m, d), F32),
        compiler_params=_cparams("parallel"),
        name="final_norm",
    )(x, gain.reshape(1, d))


def _rotary_tables(pos, rot_dim):
    inv_freq = jnp.power(ROPE_THETA, -jnp.arange(0, rot_dim, 2, dtype=F32) / rot_dim)
    ang = pos.astype(F32)[:, None] * inv_freq[None, :]
    cos, sin = jnp.cos(ang), jnp.sin(ang)
    n = pos.shape[0]
    cos_t = jnp.concatenate([cos, cos, jnp.ones((n, LANES - rot_dim), F32)], axis=-1)
    sin_t = jnp.concatenate([-sin, sin, jnp.zeros((n, LANES - rot_dim), F32)], axis=-1)
    return cos_t, sin_t


def _rotate(t, cos_t, sin_t, rot_dim):
    half = rot_dim // 2
    lane = lax.broadcasted_iota(jnp.int32, (t.shape[0], LANES), 1)
    segs = []
    for s in range(t.shape[1] // LANES):
        xs = t[:, s * LANES:(s + 1) * LANES]
        up = pltpu.roll(xs, LANES - half, 1)
        down = pltpu.roll(xs, half, 1)
        partner = jnp.where(lane < half, up, down)
        segs.append(jnp.where(lane < rot_dim, xs * cos_t + partner * sin_t, xs))
    return jnp.concatenate(segs, axis=1)


def _qkv_kernel(x_ref, g_ref, wq_ref, wk_ref, wv_ref, cos_ref, sin_ref, *refs, rot_dim):
    qb_ref, kf_ref, vf_ref, h_ref = refs[-4:]

    @pl.when(pl.program_id(1) == 0)
    def _():
        h_ref[...] = _rms(x_ref[...], g_ref[...]).astype(BF16)

    h = h_ref[...]
    cos_t, sin_t = cos_ref[...], sin_ref[...]
    qb_ref[...] = _rotate(_dot(h, wq_ref[...]), cos_t, sin_t, rot_dim).astype(BF16)
    kf_ref[...] = _rotate(_dot(h, wk_ref[...]), cos_t, sin_t, rot_dim)
    vf_ref[...] = _dot(h, wv_ref[...])


def qkv_project(x, gain, w_qkv, layer, kv_stacks, cos_t, sin_t, *, tm, tn, rot_dim):
    m, d = x.shape
    n_layers = w_qkv.shape[0]
    da = w_qkv.shape[-1] // 3
    tm = min(tm, m)
    nj = da // tn
    ntab = cos_t.shape[0] // tm
    wspec = lambda sec: _layer_spec((d, tn), lambda i, j: (0, sec * nj + j), (layer,))
    tspec = pl.BlockSpec((tm, LANES), lambda i, j: (i % ntab, 0))
    qspec = pl.BlockSpec((tm, tn), lambda i, j: (i, j))
    kvspec = _layer_spec((tm, tn), lambda i, j: (i, j), (layer,))
    stacks = () if kv_stacks is None else tuple(kv_stacks)
    n_in = 7
    return pl.pallas_call(
        functools.partial(_qkv_kernel, rot_dim=rot_dim),
        grid=(m // tm, nj),
        in_specs=[
            pl.BlockSpec((tm, d), lambda i, j: (i, 0)),
            pl.BlockSpec((1, d), lambda i, j: (0, 0)),
            wspec(0), wspec(1), wspec(2), tspec, tspec,
        ] + [pl.BlockSpec(memory_space=pl.ANY)] * len(stacks),
        out_specs=[qspec, kvspec, kvspec],
        out_shape=[jax.ShapeDtypeStruct((m, da), BF16)] + [jax.ShapeDtypeStruct((n_layers, m, da), F32)] * 2,
        input_output_aliases={n_in + t: 1 + t for t in range(len(stacks))},
        scratch_shapes=[pltpu.VMEM((tm, d), BF16)],
        compiler_params=_cparams("parallel", "arbitrary"),
        name="qkv_project",
    )(x, gain.reshape(1, d), w_qkv, w_qkv, w_qkv, cos_t, sin_t, *stacks)


def _lambda(lq1_ref, lk1_ref, lq2_ref, lk2_ref, lam_init):
    e1 = jnp.exp(jnp.sum(lq1_ref[...] * lk1_ref[...], axis=-1, keepdims=True))
    e2 = jnp.exp(jnp.sum(lq2_ref[...] * lk2_ref[...], axis=-1, keepdims=True))
    return e1 - e2 + lam_init


def _attn_kernel(q_ref, k_ref, v_ref, lq1_ref, lk1_ref, lq2_ref, lk2_ref, sub_ref,
                 o_ref, kb_ref, vb_ref, m_ref, l_ref, acc_ref, *, tq, tk, td, hd, lam_init):
    i = pl.program_id(2)
    c_scale = (hd ** -0.5) * LOG2_E
    vd = 2 * hd

    @pl.when(i == 0)
    def _():
        kb_ref[...] = k_ref[...].astype(BF16)
        vb_ref[...] = v_ref[...].astype(BF16)

    m_ref[...] = jnp.full_like(m_ref, -jnp.inf)
    l_ref[...] = jnp.zeros_like(l_ref)
    acc_ref[...] = jnp.zeros_like(acc_ref)

    def kv_block(col0, width, row0, masked):
        k = kb_ref[pl.ds(col0, width), :]
        v = vb_ref[pl.ds(col0, width), :]
        q = q_ref[row0:, :]
        if masked:
            visible = (lax.broadcasted_iota(jnp.int32, (tq - row0, width), 1)
                       <= lax.broadcasted_iota(jnp.int32, (tq - row0, width), 0))
        for c in range(2):
            sc = _dot_nt(q[:, c * hd:(c + 1) * hd], k[:, c * hd:(c + 1) * hd]) * c_scale
            if masked:
                sc = jnp.where(visible, sc, -jnp.inf)
            m_old = m_ref[c, row0:, :]
            m_new = jnp.maximum(m_old, jnp.max(sc, axis=-1, keepdims=True))
            alpha = jnp.exp2(m_old - m_new)
            p = jnp.exp2(sc - jnp.tile(m_new, (1, width // LANES)))
            l_ref[c, row0:, :] = alpha * l_ref[c, row0:, :] + jnp.sum(p, axis=-1, keepdims=True)
            acc_ref[c, row0:, :] = (jnp.tile(alpha, (1, vd // LANES)) * acc_ref[c, row0:, :]
                                    + _dot(p.astype(BF16), v))
            m_ref[c, row0:, :] = m_new

    def body(j, carry):
        kv_block(pl.multiple_of(j * tk, tk), tk, 0, False)
        return carry

    lax.fori_loop(0, (i * tq) // tk, body, 0)
    for dj in range(tq // td):
        kv_block(pl.multiple_of(i * tq + dj * td, td), td, dj * td, True)

    lam = _lambda(lq1_ref, lk1_ref, lq2_ref, lk2_ref, lam_init)
    l0 = jnp.tile(l_ref[0], (1, vd // LANES))
    l1 = jnp.tile(l_ref[1], (1, vd // LANES))
    o = acc_ref[0] / l0 - lam * (acc_ref[1] / l1)
    o_ref[...] = (_rms(o, sub_ref[...]) * (1.0 - lam_init)).astype(BF16)


def diff_attention_prompt(q, k, v, layer, lam_params, subln, *, batch, heads, tq, tk, td, lam_init):
    m, da = q.shape
    vd = da // heads
    hd = vd // 2
    seq = m // batch
    tq, tk, td = min(tq, seq), min(tk, seq), min(td, seq)
    assert tq % tk == 0 and tq % td == 0 and seq % tq == 0
    nq = seq // tq
    qspec = pl.BlockSpec((tq, vd), lambda b, h, i: (b * nq + i, h))
    kspec = _layer_spec((seq, vd), lambda b, h, i: (b, h), (layer,))
    pspec = pl.BlockSpec((1, hd), lambda b, h, i: (0, 0))
    return pl.pallas_call(
        functools.partial(_attn_kernel, tq=tq, tk=tk, td=td, hd=hd, lam_init=lam_init),
        grid=(batch, heads, nq),
        in_specs=[qspec, kspec, kspec, pspec, pspec, pspec, pspec, pl.BlockSpec((1, vd), lambda b, h, i: (0, 0))],
        out_specs=qspec,
        out_shape=jax.ShapeDtypeStruct((m, da), BF16),
        scratch_shapes=[pltpu.VMEM((seq, vd), BF16), pltpu.VMEM((seq, vd), BF16),
                        pltpu.VMEM((2, tq, LANES), F32), pltpu.VMEM((2, tq, LANES), F32),
                        pltpu.VMEM((2, tq, vd), F32)],
        compiler_params=_cparams("parallel", "parallel", "arbitrary"),
        name="diff_attention_prompt",
    )(q, k, v, *[p.reshape(1, hd) for p in lam_params], subln.reshape(1, vd))


def _decode_kernel(pt_ref, q_ref, *refs, heads, hd, lam_init, pps):
    kc_refs, vc_refs = refs[:pps], refs[pps:2 * pps]
    (kn_ref, vn_ref, lq1_ref, lk1_ref, lq2_ref, lk2_ref, sub_ref, o_ref, m_ref, l_ref, acc_ref) = refs[2 * pps:]
    p = pl.program_id(1)
    rows = 2 * heads
    c_scale = (hd ** -0.5) * LOG2_E

    @pl.when(p == 0)
    def _():
        m_ref[...] = jnp.full_like(m_ref, -jnp.inf)
        l_ref[...] = jnp.zeros_like(l_ref)
        acc_ref[...] = jnp.zeros_like(acc_ref)

    q = q_ref[...]
    lane = lax.broadcasted_iota(jnp.int32, q.shape, 1)
    qmat = jnp.concatenate([jnp.where(lane < hd, q, 0.0), jnp.where(lane >= hd, q, 0.0)], axis=0).astype(BF16)

    def online_update(scs, pv_of):
        m_old = m_ref[...]
        m_new = m_old
        for sc in scs:
            m_new = jnp.maximum(m_new, jnp.max(sc, axis=-1, keepdims=True))
        alpha = jnp.exp2(m_old - m_new)
        l_new = alpha * l_ref[...]
        acc = alpha * acc_ref[...]
        for r, sc in enumerate(scs):
            pr = jnp.exp2(sc - m_new)
            l_new = l_new + jnp.sum(pr, axis=-1, keepdims=True)
            acc = acc + pv_of(r, pr.astype(BF16))
        l_ref[...] = l_new
        acc_ref[...] = acc
        m_ref[...] = m_new

    n = kc_refs[0].shape[0]
    rh = lax.broadcasted_iota(jnp.int32, (rows, n), 0) & (heads - 1)
    ch = lax.broadcasted_iota(jnp.int32, (rows, n), 1) & (heads - 1)
    same_head = rh == ch
    scs = [jnp.where(same_head, _dot_nt(qmat, kc[...].astype(BF16)) * c_scale, -jnp.inf) for kc in kc_refs]
    online_update(scs, lambda r, pr: _dot(pr, vc_refs[r][...].astype(BF16)))

    @pl.when(p == pl.num_programs(1) - 1)
    def _():
        kn = kn_ref[...].astype(BF16).astype(F32)
        vn = vn_ref[...].astype(BF16).astype(F32)
        kn2 = jnp.concatenate([kn, kn], axis=0)
        vn2 = jnp.concatenate([vn, vn], axis=0)
        sc_new = jnp.sum(qmat.astype(F32) * kn2, axis=-1, keepdims=True) * c_scale
        online_update([sc_new], lambda r, pr: pr.astype(F32) * vn2)
        lam = _lambda(lq1_ref, lk1_ref, lq2_ref, lk2_ref, lam_init)
        on = acc_ref[...] / l_ref[...]
        o = on[:heads] - lam * on[heads:]
        o_ref[...] = _rms(o, sub_ref[...]) * (1.0 - lam_init)


def diff_attention_decode(q, k_new, v_new, cache_k, cache_v, layer, page_table, lam_params, subln,
                          *, heads, lam_init, pages_per_step):
    bs, _, vd = q.shape
    hd = vd // 2
    assert heads & (heads - 1) == 0, "same-head mask uses a power-of-two head count"
    n_layers, n_phys, page, _, _ = cache_k.shape
    n_pages = page_table.shape[1]
    pps = math.gcd(n_pages, pages_per_step)
    ck = cache_k.reshape(n_layers, n_phys, page * heads, vd)
    cv = cache_v.reshape(n_layers, n_phys, page * heads, vd)
    tok = pl.BlockSpec((None, heads, vd), lambda b, p, pt: (b, 0, 0))
    cspec = lambda r: pl.BlockSpec((None, None, page * heads, vd),
                                   lambda b, p, pt: (layer, pt[b * n_pages + p * pps + r], 0, 0))
    pspec = pl.BlockSpec((1, hd), lambda b, p, pt: (0, 0))
    return pl.pallas_call(
        functools.partial(_decode_kernel, heads=heads, hd=hd, lam_init=lam_init, pps=pps),
        grid_spec=pltpu.PrefetchScalarGridSpec(
            num_scalar_prefetch=1,
            grid=(bs, n_pages // pps),
            in_specs=[tok] + [cspec(r) for r in range(pps)] * 2 + [tok, tok, pspec, pspec, pspec, pspec,
                      pl.BlockSpec((1, vd), lambda b, p, pt: (0, 0))],
            out_specs=tok,
            scratch_shapes=[pltpu.VMEM((2 * heads, 1), F32), pltpu.VMEM((2 * heads, 1), F32),
                            pltpu.VMEM((2 * heads, vd), F32)],
        ),
        out_shape=jax.ShapeDtypeStruct((bs, heads, vd), F32),
        compiler_params=_cparams("parallel", "arbitrary"),
        name="diff_attention_decode",
    )(page_table.reshape(-1), q, *([ck] * pps), *([cv] * pps), k_new, v_new,
      *[p.reshape(1, hd) for p in lam_params], subln.reshape(1, vd))


def _log_sigmoid(x):
    return jnp.minimum(x, 0.0) - jnp.log1p(jnp.exp(-jnp.abs(x)))


def _gla_gate_kernel(x_ref, g_ref, w1_ref, w2_ref, b_ref, o_ref, *, gate_norm):
    h = _rms(x_ref[...], g_ref[...]).astype(BF16)
    low = _dot(h, w1_ref[...]).astype(BF16)
    gk = _dot(low, w2_ref[...]) + b_ref[...]
    o_ref[...] = _log_sigmoid(gk) / gate_norm


def gla_gate(x, gain, w1, w2, bias, *, tm, gate_norm):
    m, d = x.shape
    r = w1.shape[1]
    n = w2.shape[1]
    tm = min(tm, m)
    full = lambda shape: pl.BlockSpec(shape, lambda i: (0, 0))
    return pl.pallas_call(
        functools.partial(_gla_gate_kernel, gate_norm=gate_norm),
        grid=(m // tm,),
        in_specs=[pl.BlockSpec((tm, d), lambda i: (i, 0)), full((1, d)), full((d, r)), full((r, n)), full((1, n))],
        out_specs=pl.BlockSpec((tm, n), lambda i: (i, 0)),
        out_shape=jax.ShapeDtypeStruct((m, n), F32),
        compiler_params=_cparams("parallel"),
        name="gla_gate",
    )(x, gain.reshape(1, d), w1, w2, bias.reshape(1, n))


def _split3(x):
    hi = x.astype(BF16)
    r1 = x - hi.astype(F32)
    mid = r1.astype(BF16)
    lo = (r1 - mid.astype(F32)).astype(BF16)
    return hi, mid, lo


def _gla_chunk_kernel(q_ref, k_ref, v_ref, og_ref, la_ref, ng_ref, y_ref, s_ref, st_ref, *, chunk, heads, span):
    r = pl.program_id(1)

    @pl.when(r == 0)
    def _():
        st_ref[...] = jnp.zeros_like(st_ref)

    rows = q_ref.shape[0]
    dk = q_ref.shape[1] // heads
    dv = v_ref.shape[1] // heads
    shift = chunk.bit_length() - 1
    ti = lax.broadcasted_iota(jnp.int32, (chunk, chunk), 0)
    si = lax.broadcasted_iota(jnp.int32, (chunk, chunk), 1)
    tri = (si <= ti).astype(BF16)
    tb = lax.broadcasted_iota(jnp.int32, (span, span), 0)
    sb = lax.broadcasted_iota(jnp.int32, (span, span), 1)
    causal = ((tb >> shift) == (sb >> shift)) & (sb <= tb)

    bs, bls = [], []
    for c in range(rows // chunk):
        hi, mid, lo = _split3(la_ref[c * chunk:(c + 1) * chunk, :])
        b_c = _dot(tri, hi) + _dot(tri, mid) + _dot(tri, lo)
        bs.append(b_c)
        bls.append(jnp.broadcast_to(b_c[chunk - 1:chunk, :], b_c.shape))
    b = jnp.concatenate(bs, axis=0)
    b_last = jnp.concatenate(bls, axis=0)
    k = k_ref[...]
    q_e = (q_ref[...] * (dk ** -0.5) * jnp.exp(b)).astype(BF16)
    k_e = (k * jnp.exp(-b)).astype(BF16)
    k_tail = (k * jnp.exp(b_last - b)).astype(BF16)
    decay = jnp.exp(b_last)
    v = v_ref[...].astype(BF16)
    ng = ng_ref[...]
    for h in range(heads):
        ks = slice(h * dk, (h + 1) * dk)
        vs = slice(h * dv, (h + 1) * dv)
        intra = []
        for u in range(rows // span):
            us = slice(u * span, (u + 1) * span)
            attn = jnp.where(causal, _dot_nt(q_e[us, ks], k_e[us, ks]), 0.0).astype(BF16)
            intra.append(_dot(attn, v[us, vs]))
        st = st_ref[h]
        inter = []
        for c in range(rows // chunk):
            sl = slice(c * chunk, (c + 1) * chunk)
            inter.append(_dot_nt(q_e[sl, ks], st.astype(BF16)))
            st = decay[c * chunk:c * chunk + 1, ks] * st + _dot_tn(v[sl, vs], k_tail[sl, ks])
        st_ref[h] = st
        o = jnp.concatenate(intra, axis=0) + jnp.concatenate(inter, axis=0)
        y_ref[:, vs] = (_rms(o, ng) * _silu(og_ref[:, vs])).astype(BF16)

    @pl.when(r == pl.num_programs(1) - 1)
    def _():
        for h in range(heads):
            s_ref[h] = st_ref[h].T


def gla_prompt_core(proj, log_a, norm_g, *, batch, heads, rows, chunk):
    m = proj.shape[0]
    kd = log_a.shape[1]
    vd = (proj.shape[1] - 2 * kd) // 2
    dk, dv = kd // heads, vd // heads
    seq = m // batch
    rows = min(rows, seq)
    assert chunk & (chunk - 1) == 0 and rows % chunk == 0, "chunk ids come from a shift"
    nr = seq // rows
    rowblk = lambda col0, width: pl.BlockSpec((rows, width), lambda b, r: (b * nr + r, col0 // width))
    return pl.pallas_call(
        functools.partial(_gla_chunk_kernel, chunk=chunk, heads=heads, span=min(rows, GLA_INTRA_SPAN)),
        grid=(batch, nr),
        in_specs=[rowblk(0, kd), rowblk(kd, kd), rowblk(2 * kd, vd), rowblk(2 * kd + vd, vd),
                  pl.BlockSpec((rows, kd), lambda b, r: (b * nr + r, 0)),
                  pl.BlockSpec((1, dv), lambda b, r: (0, 0))],
        out_specs=[pl.BlockSpec((rows, vd), lambda b, r: (b * nr + r, 0)),
                   pl.BlockSpec((None, heads, dk, dv), lambda b, r: (b, 0, 0, 0))],
        out_shape=[jax.ShapeDtypeStruct((m, vd), BF16), jax.ShapeDtypeStruct((batch, heads, dk, dv), F32)],
        scratch_shapes=[pltpu.VMEM((heads, dv, dk), F32)],
        compiler_params=_cparams("parallel", "arbitrary"),
        name="gla_prompt_core",
    )(proj, proj, proj, proj, log_a, norm_g.reshape(1, dv))


def _gla_step_kernel(q_ref, kcol_ref, gcol_ref, v_ref, og_ref, s0_ref, ng_ref, y_ref, s_ref):
    heads, dk, _ = s0_ref.shape
    for h in range(heads):
        s_new = jnp.exp(gcol_ref[h]) * s0_ref[h] + kcol_ref[h] * v_ref[h]
        s_ref[h] = s_new
        q = jnp.broadcast_to(q_ref[h] * (dk ** -0.5), (8, dk)).astype(BF16)
        o = _dot(q, s_new.astype(BF16))[0:1, :]
        y_ref[h] = _rms(o, ng_ref[...]) * _silu(og_ref[h])


def gla_decode_step(q, k, log_a, v, og, state, norm_g):
    bs, heads, dk = q.shape
    dv = v.shape[-1]
    row = lambda w: pl.BlockSpec((None, heads, 1, w), lambda b: (b, 0, 0, 0))
    col = pl.BlockSpec((None, heads, dk, 1), lambda b: (b, 0, 0, 0))
    mat = pl.BlockSpec((None, heads, dk, dv), lambda b: (b, 0, 0, 0))
    y, s = pl.pallas_call(
        _gla_step_kernel,
        grid=(bs,),
        in_specs=[row(dk), col, col, row(dv), row(dv), mat, pl.BlockSpec((1, dv), lambda b: (0, 0))],
        out_specs=[row(dv), mat],
        out_shape=[jax.ShapeDtypeStruct((bs, heads, 1, dv), F32), jax.ShapeDtypeStruct((bs, heads, dk, dv), F32)],
        compiler_params=_cparams("parallel"),
        name="gla_decode_step",
    )(q[:, :, None, :], k[..., None], log_a[..., None], v[:, :, None, :], og[:, :, None, :], state,
      norm_g.reshape(1, dv))
    return y.reshape(bs, heads * dv), s


def _pad_axis(a, axis, mult):
    pad = (-a.shape[axis]) % mult
    if pad == 0:
        return a
    widths = [(0, 0)] * a.ndim
    widths[axis] = (0, pad)
    return jnp.pad(a, widths)


def _forward(x_prompt, x_sample, cache_k, cache_v, state_gla, page_table, norm_sub, final_norm_g,
             w_ffn_gate, w_ffn_up, w_ffn_down, w_qkv_a, lambda_q1, lambda_k1, lambda_q2, lambda_k2,
             subln_a, w_o_a, w_proj_g, w_gk1, w_gk2, b_gk, norm_g, w_o_g, cfg):
    batch, seq, d = x_prompt.shape
    bs = x_sample.shape[0]
    depth = norm_sub.shape[0]
    heads, gheads = cfg["attn_heads"], cfg["gla_heads"]
    n_past = page_table.shape[1] * cache_k.shape[2]
    da = w_qkv_a.shape[2] // 3
    vd_attn = da // heads
    kd = w_gk2.shape[2]
    vd = (w_proj_g.shape[2] - 2 * kd) // 2
    dk, dv = kd // gheads, vd // gheads
    bs_pad = -(-bs // BF16_SUBLANES) * BF16_SUBLANES

    xp = x_prompt.reshape(batch * seq, d)
    xs = _pad_axis(x_sample.reshape(bs, d), 0, BF16_SUBLANES)

    tf = cfg["ffn_tf"]
    wqkv = w_qkv_a.astype(BF16)
    woa = w_o_a.astype(BF16)
    wproj = w_proj_g.astype(BF16)
    wog = w_o_g.astype(BF16)
    wgk1 = _pad_axis(w_gk1.astype(BF16), 2, LANES)
    wgk2 = _pad_axis(w_gk2.astype(BF16), 1, LANES)

    cos_p, sin_p = _rotary_tables(jnp.arange(seq), cfg["rot_dim"])
    cos_s, sin_s = _rotary_tables(jnp.full((bs_pad,), n_past, jnp.int32), cfg["rot_dim"])

    def ffn(xp, xs, i, half):
        gain = norm_sub[i, 2 * half]
        xs, wg, wu, wd = ffn_half_cast(xs, gain, w_ffn_gate, w_ffn_up, w_ffn_down, (i, half), tf=tf)
        last = i == depth - 1 and half == 1
        return ffn_half(xp, gain, wg, wu, wd, (), tm=cfg["ffn_tm"], tf=tf,
                        out_gain=final_norm_g if last else None), xs

    proj_t = dict(tm=cfg["proj_tm"], tn=cfg["proj_tn"])
    qkv_t = dict(tm=cfg["qkv_tm"], tn=cfg["qkv_tn"], rot_dim=cfg["rot_dim"])

    kv_p, kv_s, st_p, st_s = None, None, [], []
    for i in range(depth):
        j = i // cfg["n_mixers"]
        xp, xs = ffn(xp, xs, i, 0)
        if i % cfg["n_mixers"] == 0:
            lam_init = 0.8 - 0.6 * math.exp(-0.3 * i)
            lam_params = (lambda_q1[j], lambda_k1[j], lambda_q2[j], lambda_k2[j])
            qb, *kv_p = qkv_project(xp, norm_sub[i, 1], wqkv, j, kv_p, cos_p, sin_p, **qkv_t)
            a = diff_attention_prompt(qb, *kv_p, j, lam_params, subln_a[j], batch=batch, heads=heads,
                                      tq=cfg["attn_tq"], tk=cfg["attn_tk"], td=cfg["attn_td"], lam_init=lam_init)
            xp = matmul_residual(a, woa, xp, (j,), tm=cfg["out_tm"])

            qb, *kv_s = qkv_project(xs, norm_sub[i, 1], wqkv, j, kv_s, cos_s, sin_s, **qkv_t)
            tok = lambda t: t[:bs].reshape(bs, heads, vd_attn)
            o = diff_attention_decode(tok(qb).astype(F32), tok(kv_s[0][j]), tok(kv_s[1][j]), cache_k, cache_v, j,
                                      page_table, lam_params, subln_a[j], heads=heads, lam_init=lam_init,
                                      pages_per_step=cfg["decode_pages"])
            a = _pad_axis(o.reshape(bs, da), 0, BF16_SUBLANES).astype(BF16)
            xs = matmul_residual(a, woa, xs, (j,), tm=cfg["out_tm"])
        else:
            gate = functools.partial(gla_gate, tm=cfg["proj_tm"], gate_norm=cfg["gla_gate_norm"])
            proj = norm_matmul(xp, norm_sub[i, 1], wproj, (j,), **proj_t)
            log_a = gate(xp, norm_sub[i, 1], wgk1[j], wgk2[j], b_gk[j])
            y, s_fin = gla_prompt_core(proj, log_a, norm_g[j], batch=batch, heads=gheads,
                                       rows=cfg["gla_rows"], chunk=cfg["gla_chunk"])
            st_p.append(s_fin)
            xp = matmul_residual(y, wog, xp, (j,), tm=cfg["out_tm"])

            proj = norm_matmul(xs, norm_sub[i, 1], wproj, (j,), **proj_t)[:bs]
            log_a = gate(xs, norm_sub[i, 1], wgk1[j], wgk2[j], b_gk[j])[:bs]
            y, s_new = gla_decode_step(
                proj[:, :kd].reshape(bs, gheads, dk), proj[:, kd:2 * kd].reshape(bs, gheads, dk),
                log_a.reshape(bs, gheads, dk), proj[:, 2 * kd:2 * kd + vd].reshape(bs, gheads, dv),
                proj[:, 2 * kd + vd:].reshape(bs, gheads, dv), state_gla[j], norm_g[j])
            st_s.append(s_new)
            xs = matmul_residual(_pad_axis(y, 0, BF16_SUBLANES).astype(BF16), wog, xs, (j,), tm=cfg["out_tm"])
        xp, xs = ffn(xp, xs, i, 1)

    yp = xp.reshape(batch, seq, d)
    ys = final_norm(xs, final_norm_g, tm=cfg["out_tm"])[:bs].reshape(bs, 1, d)
    n_attn = w_qkv_a.shape[0]
    k_p, v_p = (t.reshape(n_attn, batch, seq, heads, vd_attn) for t in kv_p)
    k_s, v_s = (t[:, :bs].reshape(n_attn, bs, 1, heads, vd_attn) for t in kv_s)
    return yp, ys, k_p, v_p, jnp.stack(st_p), k_s, v_s, jnp.stack(st_s)


def kernel(x_prompt, x_sample, cache_k, cache_v, state_gla, page_table, norm_sub, final_norm, w_ffn_gate, w_ffn_up, w_ffn_down, w_qkv_a, lambda_q1, lambda_k1, lambda_q2, lambda_k2, subln_a, w_o_a, w_proj_g, w_gk1, w_gk2, b_gk, norm_g, w_o_g):
    return _forward(x_prompt, x_sample, cache_k, cache_v, state_gla, page_table, norm_sub, final_norm,
                    w_ffn_gate, w_ffn_up, w_ffn_down, w_qkv_a, lambda_q1, lambda_k1, lambda_q2, lambda_k2,
                    subln_a, w_o_a, w_proj_g, w_gk1, w_gk2, b_gk, norm_g, w_o_g, CFG)
```

```python
import functools
import math

import jax
import jax.numpy as jnp
from jax import lax
from jax.experimental import pallas as pl
from jax.experimental.pallas import tpu as pltpu

F32 = jnp.float32
BF16 = jnp.bfloat16

RMS_EPS = 1e-6
ROPE_THETA = 500000.0
FFN_RESIDUAL_SCALE = 0.5
LOG2_E = 1.4426950408889634
LANES = 128
BF16_SUBLANES = 16
VMEM_LIMIT_BYTES = 56 * 1024 * 1024
GLA_INTRA_SPAN = 256

CFG = dict(
    attn_heads=8, gla_heads=4, rot_dim=32, gla_gate_norm=16.0, gla_chunk=64, n_mixers=2,
    ffn_tm=1024, ffn_tf=512, proj_tm=1024, proj_tn=1024, qkv_tm=1024, qkv_tn=512, out_tm=512,
    attn_tq=1024, attn_tk=512, attn_td=256, decode_pages=8, gla_rows=512,
)


def _cparams(*sem):
    return pltpu.CompilerParams(dimension_semantics=sem, vmem_limit_bytes=VMEM_LIMIT_BYTES)


def _rms(x, gain):
    return x * lax.rsqrt(jnp.mean(x * x, axis=-1, keepdims=True) + RMS_EPS) * gain


def _silu(x):
    return x * jax.nn.sigmoid(x)


def _dot(a, b):
    return jnp.dot(a, b, preferred_element_type=F32)


def _dot_nt(a, b):
    return lax.dot_general(a, b, (((1,), (1,)), ((), ())), preferred_element_type=F32)


def _dot_tn(a, b):
    return lax.dot_general(a, b, (((0,), (0,)), ((), ())), preferred_element_type=F32)


def _layer_spec(block, index_map, lead):
    return pl.BlockSpec((None,) * len(lead) + block, lambda *g: tuple(lead) + tuple(index_map(*g)))


def _ffn_kernel(x_ref, g_ref, wg_ref, wu_ref, wd_ref, *refs):
    o_ref, h_ref = refs[-2:]

    @pl.when(pl.program_id(1) == 0)
    def _():
        x = x_ref[...]
        h_ref[...] = _rms(x, g_ref[...]).astype(BF16)
        o_ref[...] = x

    h = h_ref[...]
    a = (_silu(_dot(h, wg_ref[...])) * (FFN_RESIDUAL_SCALE * _dot(h, wu_ref[...]))).astype(BF16)
    o_ref[...] += _dot(a, wd_ref[...])

    if len(refs) == 3:
        @pl.when(pl.program_id(1) == pl.num_programs(1) - 1)
        def _():
            o_ref[...] = _rms(o_ref[...], refs[0][...])


def ffn_half(x, gain, wg, wu, wd, lead, *, tm, tf, out_gain=None):
    m, d = x.shape
    ffp = wg.shape[-1]
    tm = min(tm, m)
    vec = pl.BlockSpec((1, d), lambda i, f: (0, 0))
    extra = [] if out_gain is None else [out_gain.reshape(1, d)]
    return pl.pallas_call(
        _ffn_kernel,
        grid=(m // tm, ffp // tf),
        in_specs=[
            pl.BlockSpec((tm, d), lambda i, f: (i, 0)),
            vec,
            _layer_spec((d, tf), lambda i, f: (0, f), lead),
            _layer_spec((d, tf), lambda i, f: (0, f), lead),
            _layer_spec((tf, d), lambda i, f: (f, 0), lead),
        ] + [vec] * len(extra),
        out_specs=pl.BlockSpec((tm, d), lambda i, f: (i, 0)),
        out_shape=jax.ShapeDtypeStruct((m, d), F32),
        scratch_shapes=[pltpu.VMEM((tm, d), BF16)],
        compiler_params=_cparams("parallel", "arbitrary"),
        name="ffn_half",
    )(x, gain.reshape(1, d), wg, wu, wd, *extra)


def _ffn_cast_kernel(x_ref, g_ref, wg_ref, wu_ref, wd_ref, o_ref, wgb_ref, wub_ref, wdb_ref, h_ref, *, ff):
    f = pl.program_id(0)
    tf = wg_ref.shape[1]
    valid = ff - f * tf
    col_ok = lax.broadcasted_iota(jnp.int32, (1, tf), 1) < valid
    row_ok = lax.broadcasted_iota(jnp.int32, (tf, 1), 0) < valid
    wg = jnp.where(col_ok, wg_ref[...], 0.0).astype(BF16)
    wu = jnp.where(col_ok, wu_ref[...], 0.0).astype(BF16)
    wd = jnp.where(row_ok, wd_ref[...], 0.0).astype(BF16)
    wgb_ref[...] = wg
    wub_ref[...] = wu
    wdb_ref[...] = wd

    @pl.when(f == 0)
    def _():
        x = x_ref[...]
        h_ref[...] = _rms(x, g_ref[...]).astype(BF16)
        o_ref[...] = x

    h = h_ref[...]
    a = (_silu(_dot(h, wg)) * (FFN_RESIDUAL_SCALE * _dot(h, wu))).astype(BF16)
    o_ref[...] += _dot(a, wd)


def ffn_half_cast(x, gain, wg, wu, wd, lead, *, tf):
    m, d = x.shape
    ff = wg.shape[-1]
    nf = pl.cdiv(ff, tf)
    ffp = nf * tf
    return pl.pallas_call(
        functools.partial(_ffn_cast_kernel, ff=ff),
        grid=(nf,),
        in_specs=[
            pl.BlockSpec((m, d), lambda f: (0, 0)),
            pl.BlockSpec((1, d), lambda f: (0, 0)),
            _layer_spec((d, tf), lambda f: (0, f), lead),
            _layer_spec((d, tf), lambda f: (0, f), lead),
            _layer_spec((tf, d), lambda f: (f, 0), lead),
        ],
        out_specs=[
            pl.BlockSpec((m, d), lambda f: (0, 0)),
            pl.BlockSpec((d, tf), lambda f: (0, f)),
            pl.BlockSpec((d, tf), lambda f: (0, f)),
            pl.BlockSpec((tf, d), lambda f: (f, 0)),
        ],
        out_shape=[jax.ShapeDtypeStruct((m, d), F32), jax.ShapeDtypeStruct((d, ffp), BF16),
                   jax.ShapeDtypeStruct((d, ffp), BF16), jax.ShapeDtypeStruct((ffp, d), BF16)],
        scratch_shapes=[pltpu.VMEM((m, d), BF16)],
        compiler_params=_cparams("arbitrary"),
        name="ffn_half_cast",
    )(x, gain.reshape(1, d), wg, wu, wd)


def _norm_matmul_kernel(x_ref, g_ref, w_ref, o_ref, *refs):
    h_ref = refs[-1]

    @pl.when(pl.program_id(1) == 0)
    def _():
        h_ref[...] = _rms(x_ref[...], g_ref[...]).astype(BF16)

    w = w_ref[...].astype(BF16)
    if len(refs) == 2:
        refs[0][...] = w
    o_ref[...] = _dot(h_ref[...], w)


def norm_matmul(x, gain, w, lead, *, tm, tn):
    m, d = x.shape
    n = w.shape[-1]
    tm = min(tm, m)
    emit = w.dtype != BF16
    assert not emit or m == tm
    ospec = pl.BlockSpec((tm, tn), lambda i, j: (i, j))
    out = pl.pallas_call(
        _norm_matmul_kernel,
        grid=(m // tm, n // tn),
        in_specs=[
            pl.BlockSpec((tm, d), lambda i, j: (i, 0)),
            pl.BlockSpec((1, d), lambda i, j: (0, 0)),
            _layer_spec((d, tn), lambda i, j: (0, j), lead),
        ],
        out_specs=[ospec] + [pl.BlockSpec((d, tn), lambda i, j: (0, j))] * emit,
        out_shape=[jax.ShapeDtypeStruct((m, n), F32)] + [jax.ShapeDtypeStruct((d, n), BF16)] * emit,
        scratch_shapes=[pltpu.VMEM((tm, d), BF16)],
        compiler_params=_cparams("parallel", "arbitrary"),
        name="norm_matmul",
    )(x, gain.reshape(1, d), w)
    return out if emit else out[0]


def _matmul_residual_kernel(a_ref, w_ref, x_ref, o_ref, *wb_ref):
    w = w_ref[...].astype(BF16)
    if wb_ref:
        wb_ref[0][...] = w
    o_ref[...] = x_ref[...] + _dot(a_ref[...], w)


def matmul_residual(a, w, x, lead, *, tm, tn=None):
    m, k = a.shape
    n = w.shape[-1]
    tm = min(tm, m)
    emit = w.dtype != BF16
    tn = tn if emit else n
    assert not emit or m == tm
    xspec = pl.BlockSpec((tm, tn), lambda i, j: (i, j))
    out = pl.pallas_call(
        _matmul_residual_kernel,
        grid=(m // tm, n // tn),
        in_specs=[
            pl.BlockSpec((tm, k), lambda i, j: (i, 0)),
            _layer_spec((k, tn), lambda i, j: (0, j), lead),
            xspec,
        ],
        out_specs=[xspec] + [pl.BlockSpec((k, tn), lambda i, j: (0, j))] * emit,
        out_shape=[jax.ShapeDtypeStruct((m, n), F32)] + [jax.ShapeDtypeStruct((k, n), BF16)] * emit,
        compiler_params=_cparams("parallel", "arbitrary"),
        name="matmul_residual",
    )(a, w, x)
    return out if emit else out[0]


def _final_norm_kernel(x_ref, g_ref, o_ref):
    o_ref[...] = _rms(x_ref[...], g_ref[...])


def final_norm(x, gain, *, tm):
    m, d = x.shape
    tm = min(tm, m)
    return pl.pallas_call(
        _final_norm_kernel,
        grid=(m // tm,),
        in_specs=[pl.BlockSpec((tm, d), lambda i: (i, 0)), pl.BlockSpec((1, d), lambda i: (0, 0))],
        out_specs=pl.BlockSpec((tm, d), lambda i: (i, 0)),
        out_shape=jax.ShapeDtypeStruct((m, d), F32),
        compiler_params=_cparams("parallel"),
        name="final_norm",
    )(x, gain.reshape(1, d))


def _rotary_tables(pos, rot_dim):
    inv_freq = jnp.power(ROPE_THETA, -jnp.arange(0, rot_dim, 2, dtype=F32) / rot_dim)
    ang = pos.astype(F32)[:, None] * inv_freq[None, :]
    cos, sin = jnp.cos(ang), jnp.sin(ang)
    n = pos.shape[0]
    cos_t = jnp.concatenate([cos, cos, jnp.ones((n, LANES - rot_dim), F32)], axis=-1)
    sin_t = jnp.concatenate([-sin, sin, jnp.zeros((n, LANES - rot_dim), F32)], axis=-1)
    return cos_t, sin_t


def _rotate(t, cos_t, sin_t, rot_dim):
    half = rot_dim // 2
    lane = lax.broadcasted_iota(jnp.int32, (t.shape[0], LANES), 1)
    segs = []
    for s in range(t.shape[1] // LANES):
        xs = t[:, s * LANES:(s + 1) * LANES]
        up = pltpu.roll(xs, LANES - half, 1)
        down = pltpu.roll(xs, half, 1)
        partner = jnp.where(lane < half, up, down)
        segs.append(jnp.where(lane < rot_dim, xs * cos_t + partner * sin_t, xs))
    return jnp.concatenate(segs, axis=1)


def _qkv_kernel(x_ref, g_ref, wq_ref, wk_ref, wv_ref, cos_ref, sin_ref, kin_ref, vin_ref, *refs, rot_dim):
    del kin_ref, vin_ref
    qb_ref, kf_ref, vf_ref = refs[:3]
    h_ref = refs[-1]

    @pl.when(pl.program_id(1) == 0)
    def _():
        h_ref[...] = _rms(x_ref[...], g_ref[...]).astype(BF16)

    h = h_ref[...]
    cos_t, sin_t = cos_ref[...], sin_ref[...]
    wq, wk, wv = (w_ref[...].astype(BF16) for w_ref in (wq_ref, wk_ref, wv_ref))
    for w, wb_ref in zip((wq, wk, wv), refs[3:-1]):
        wb_ref[...] = w
    qb_ref[...] = _rotate(_dot(h, wq), cos_t, sin_t, rot_dim).astype(BF16)
    kf_ref[...] = _rotate(_dot(h, wk), cos_t, sin_t, rot_dim)
    vf_ref[...] = _dot(h, wv)


def qkv_project(x, gain, w_qkv, layer, kv_stacks, cos_t, sin_t, *, tm, tn, rot_dim):
    m, d = x.shape
    tm = min(tm, m)
    emit = not isinstance(w_qkv, tuple)
    if emit:
        assert m == tm
        da = w_qkv.shape[-1] // 3
        nj = da // tn
        ws = (w_qkv,) * 3
        wspecs = [_layer_spec((d, tn), lambda i, j, sec=sec: (0, sec * nj + j), (layer,)) for sec in range(3)]
    else:
        da = w_qkv[0].shape[-1]
        nj = da // tn
        ws = w_qkv
        wspecs = [pl.BlockSpec((d, tn), lambda i, j: (0, j))] * 3
    ntab = cos_t.shape[0] // tm
    tspec = pl.BlockSpec((tm, LANES), lambda i, j: (i % ntab, 0))
    qspec = pl.BlockSpec((tm, tn), lambda i, j: (i, j))
    kvspec = _layer_spec((tm, tn), lambda i, j: (i, j), (layer,))
    n_w = 3 * emit
    return pl.pallas_call(
        functools.partial(_qkv_kernel, rot_dim=rot_dim),
        grid=(m // tm, nj),
        in_specs=[
            pl.BlockSpec((tm, d), lambda i, j: (i, 0)),
            pl.BlockSpec((1, d), lambda i, j: (0, 0)),
            *wspecs, tspec, tspec,
            pl.BlockSpec(memory_space=pl.ANY), pl.BlockSpec(memory_space=pl.ANY),
        ],
        out_specs=[qspec, kvspec, kvspec] + [pl.BlockSpec((d, tn), lambda i, j: (0, j))] * n_w,
        out_shape=([jax.ShapeDtypeStruct((m, da), BF16)] + [jax.ShapeDtypeStruct(t.shape, F32) for t in kv_stacks]
                   + [jax.ShapeDtypeStruct((d, da), BF16)] * n_w),
        input_output_aliases={7: 1, 8: 2},
        scratch_shapes=[pltpu.VMEM((tm, d), BF16)],
        compiler_params=_cparams("parallel", "arbitrary"),
        name="qkv_project",
    )(x, gain.reshape(1, d), *ws, cos_t, sin_t, *kv_stacks)


def _lambda(lq1_ref, lk1_ref, lq2_ref, lk2_ref, lam_init):
    e1 = jnp.exp(jnp.sum(lq1_ref[...] * lk1_ref[...], axis=-1, keepdims=True))
    e2 = jnp.exp(jnp.sum(lq2_ref[...] * lk2_ref[...], axis=-1, keepdims=True))
    return e1 - e2 + lam_init


def _attn_kernel(q_ref, k_ref, v_ref, lq1_ref, lk1_ref, lq2_ref, lk2_ref, sub_ref,
                 o_ref, kb_ref, vb_ref, m_ref, l_ref, acc_ref, *, tq, tk, td, hd, lam_init):
    i = pl.program_id(2)
    c_scale = (hd ** -0.5) * LOG2_E
    vd = 2 * hd

    @pl.when(i == 0)
    def _():
        kb_ref[...] = k_ref[...].astype(BF16)
        vb_ref[...] = v_ref[...].astype(BF16)

    m_ref[...] = jnp.full_like(m_ref, -jnp.inf)
    l_ref[...] = jnp.zeros_like(l_ref)
    acc_ref[...] = jnp.zeros_like(acc_ref)

    def kv_block(col0, width, row0, masked):
        k = kb_ref[pl.ds(col0, width), :]
        v = vb_ref[pl.ds(col0, width), :]
        q = q_ref[row0:, :]
        if masked:
            visible = (lax.broadcasted_iota(jnp.int32, (tq - row0, width), 1)
                       <= lax.broadcasted_iota(jnp.int32, (tq - row0, width), 0))
        for c in range(2):
            sc = _dot_nt(q[:, c * hd:(c + 1) * hd], k[:, c * hd:(c + 1) * hd]) * c_scale
            if masked:
                sc = jnp.where(visible, sc, -jnp.inf)
            m_old = m_ref[c, row0:, :]
            m_new = jnp.maximum(m_old, jnp.max(sc, axis=-1, keepdims=True))
            alpha = jnp.exp2(m_old - m_new)
            p = jnp.exp2(sc - jnp.tile(m_new, (1, width // LANES)))
            l_ref[c, row0:, :] = alpha * l_ref[c, row0:, :] + jnp.sum(p, axis=-1, keepdims=True)
            acc_ref[c, row0:, :] = (jnp.tile(alpha, (1, vd // LANES)) * acc_ref[c, row0:, :]
                                    + _dot(p.astype(BF16), v))
            m_ref[c, row0:, :] = m_new

    def body(j, carry):
        kv_block(pl.multiple_of(j * tk, tk), tk, 0, False)
        return carry

    lax.fori_loop(0, (i * tq) // tk, body, 0)
    for dj in range(tq // td):
        kv_block(pl.multiple_of(i * tq + dj * td, td), td, dj * td, True)

    lam = _lambda(lq1_ref, lk1_ref, lq2_ref, lk2_ref, lam_init)
    l0 = jnp.tile(l_ref[0], (1, vd // LANES))
    l1 = jnp.tile(l_ref[1], (1, vd // LANES))
    o = acc_ref[0] / l0 - lam * (acc_ref[1] / l1)
    o_ref[...] = (_rms(o, sub_ref[...]) * (1.0 - lam_init)).astype(BF16)


def diff_attention_prompt(q, k, v, layer, lam_params, subln, *, batch, heads, tq, tk, td, lam_init):
    m, da = q.shape
    vd = da // heads
    hd = vd // 2
    seq = m // batch
    tq, tk, td = min(tq, seq), min(tk, seq), min(td, seq)
    assert tq % tk == 0 and tq % td == 0 and seq % tq == 0
    nq = seq // tq
    qspec = pl.BlockSpec((tq, vd), lambda b, h, i: (b * nq + i, h))
    kspec = _layer_spec((seq, vd), lambda b, h, i: (b, h), (layer,))
    pspec = pl.BlockSpec((1, hd), lambda b, h, i: (0, 0))
    return pl.pallas_call(
        functools.partial(_attn_kernel, tq=tq, tk=tk, td=td, hd=hd, lam_init=lam_init),
        grid=(batch, heads, nq),
        in_specs=[qspec, kspec, kspec, pspec, pspec, pspec, pspec, pl.BlockSpec((1, vd), lambda b, h, i: (0, 0))],
        out_specs=qspec,
        out_shape=jax.ShapeDtypeStruct((m, da), BF16),
        scratch_shapes=[pltpu.VMEM((seq, vd), BF16), pltpu.VMEM((seq, vd), BF16),
                        pltpu.VMEM((2, tq, LANES), F32), pltpu.VMEM((2, tq, LANES), F32),
                        pltpu.VMEM((2, tq, vd), F32)],
        compiler_params=_cparams("parallel", "parallel", "arbitrary"),
        name="diff_attention_prompt",
    )(q, k, v, *[p.reshape(1, hd) for p in lam_params], subln.reshape(1, vd))


def _decode_kernel(pt_ref, q_ref, *refs, heads, hd, lam_init, pps):
    kc_refs, vc_refs = refs[:pps], refs[pps:2 * pps]
    (kn_ref, vn_ref, lq1_ref, lk1_ref, lq2_ref, lk2_ref, sub_ref, o_ref, m_ref, l_ref, acc_ref) = refs[2 * pps:]
    p = pl.program_id(1)
    rows = 2 * heads
    c_scale = (hd ** -0.5) * LOG2_E

    @pl.when(p == 0)
    def _():
        m_ref[...] = jnp.full_like(m_ref, -jnp.inf)
        l_ref[...] = jnp.zeros_like(l_ref)
        acc_ref[...] = jnp.zeros_like(acc_ref)

    q = q_ref[...]
    lane = lax.broadcasted_iota(jnp.int32, q.shape, 1)
    qmat = jnp.concatenate([jnp.where(lane < hd, q, 0.0), jnp.where(lane >= hd, q, 0.0)], axis=0).astype(BF16)

    def online_update(scs, pv_of):
        m_old = m_ref[...]
        m_new = m_old
        for sc in scs:
            m_new = jnp.maximum(m_new, jnp.max(sc, axis=-1, keepdims=True))
        alpha = jnp.exp2(m_old - m_new)
        l_new = alpha * l_ref[...]
        acc = alpha * acc_ref[...]
        for r, sc in enumerate(scs):
            pr = jnp.exp2(sc - m_new)
            l_new = l_new + jnp.sum(pr, axis=-1, keepdims=True)
            acc = acc + pv_of(r, pr.astype(BF16))
        l_ref[...] = l_new
        acc_ref[...] = acc
        m_ref[...] = m_new

    n = kc_refs[0].shape[0]
    rh = lax.broadcasted_iota(jnp.int32, (rows, n), 0) & (heads - 1)
    ch = lax.broadcasted_iota(jnp.int32, (rows, n), 1) & (heads - 1)
    same_head = rh == ch
    scs = [jnp.where(same_head, _dot_nt(qmat, kc[...].astype(BF16)) * c_scale, -jnp.inf) for kc in kc_refs]
    online_update(scs, lambda r, pr: _dot(pr, vc_refs[r][...].astype(BF16)))

    @pl.when(p == pl.num_programs(1) - 1)
    def _():
        kn = kn_ref[...].astype(BF16).astype(F32)
        vn = vn_ref[...].astype(BF16).astype(F32)
        kn2 = jnp.concatenate([kn, kn], axis=0)
        vn2 = jnp.concatenate([vn, vn], axis=0)
        sc_new = jnp.sum(qmat.astype(F32) * kn2, axis=-1, keepdims=True) * c_scale
        online_update([sc_new], lambda r, pr: pr.astype(F32) * vn2)
        lam = _lambda(lq1_ref, lk1_ref, lq2_ref, lk2_ref, lam_init)
        on = acc_ref[...] / l_ref[...]
        o = on[:heads] - lam * on[heads:]
        o_ref[...] = _rms(o, sub_ref[...]) * (1.0 - lam_init)


def diff_attention_decode(q, k_new, v_new, cache_k, cache_v, layer, page_table, lam_params, subln,
                          *, heads, lam_init, pages_per_step):
    bs, _, vd = q.shape
    hd = vd // 2
    assert heads & (heads - 1) == 0, "same-head mask uses a power-of-two head count"
    n_layers, n_phys, page, _, _ = cache_k.shape
    n_pages = page_table.shape[1]
    pps = math.gcd(n_pages, pages_per_step)
    ck = cache_k.reshape(n_layers, n_phys, page * heads, vd)
    cv = cache_v.reshape(n_layers, n_phys, page * heads, vd)
    tok = pl.BlockSpec((None, heads, vd), lambda b, p, pt: (b, 0, 0))
    cspec = lambda r: pl.BlockSpec((None, None, page * heads, vd),
                                   lambda b, p, pt: (layer, pt[b * n_pages + p * pps + r], 0, 0))
    pspec = pl.BlockSpec((1, hd), lambda b, p, pt: (0, 0))
    return pl.pallas_call(
        functools.partial(_decode_kernel, heads=heads, hd=hd, lam_init=lam_init, pps=pps),
        grid_spec=pltpu.PrefetchScalarGridSpec(
            num_scalar_prefetch=1,
            grid=(bs, n_pages // pps),
            in_specs=[tok] + [cspec(r) for r in range(pps)] * 2 + [tok, tok, pspec, pspec, pspec, pspec,
                      pl.BlockSpec((1, vd), lambda b, p, pt: (0, 0))],
            out_specs=tok,
            scratch_shapes=[pltpu.VMEM((2 * heads, 1), F32), pltpu.VMEM((2 * heads, 1), F32),
                            pltpu.VMEM((2 * heads, vd), F32)],
        ),
        out_shape=jax.ShapeDtypeStruct((bs, heads, vd), F32),
        compiler_params=_cparams("parallel", "arbitrary"),
        name="diff_attention_decode",
    )(page_table.reshape(-1), q, *([ck] * pps), *([cv] * pps), k_new, v_new,
      *[p.reshape(1, hd) for p in lam_params], subln.reshape(1, vd))


def _log_sigmoid(x):
    return jnp.minimum(x, 0.0) - jnp.log1p(jnp.exp(-jnp.abs(x)))


def _gla_gate_kernel(x_ref, g_ref, w1_ref, w2_ref, b_ref, o_ref, *, gate_norm):
    h = _rms(x_ref[...], g_ref[...]).astype(BF16)
    low = _dot(h, w1_ref[...]).astype(BF16)
    gk = _dot(low, w2_ref[...]) + b_ref[...]
    o_ref[...] = _log_sigmoid(gk) / gate_norm


def gla_gate(x, gain, w1, w2, bias, *, tm, gate_norm):
    m, d = x.shape
    r = w1.shape[1]
    n = w2.shape[1]
    tm = min(tm, m)
    full = lambda shape: pl.BlockSpec(shape, lambda i: (0, 0))
    return pl.pallas_call(
        functools.partial(_gla_gate_kernel, gate_norm=gate_norm),
        grid=(m // tm,),
        in_specs=[pl.BlockSpec((tm, d), lambda i: (i, 0)), full((1, d)), full((d, r)), full((r, n)), full((1, n))],
        out_specs=pl.BlockSpec((tm, n), lambda i: (i, 0)),
        out_shape=jax.ShapeDtypeStruct((m, n), F32),
        compiler_params=_cparams("parallel"),
        name="gla_gate",
    )(x, gain.reshape(1, d), w1, w2, bias.reshape(1, n))


def _split3(x):
    hi = x.astype(BF16)
    r1 = x - hi.astype(F32)
    mid = r1.astype(BF16)
    lo = (r1 - mid.astype(F32)).astype(BF16)
    return hi, mid, lo


def _gla_chunk_kernel(q_ref, k_ref, v_ref, og_ref, la_ref, ng_ref, y_ref, s_ref, st_ref, *, chunk, heads, span):
    r = pl.program_id(1)

    @pl.when(r == 0)
    def _():
        st_ref[...] = jnp.zeros_like(st_ref)

    rows = q_ref.shape[0]
    dk = q_ref.shape[1] // heads
    dv = v_ref.shape[1] // heads
    shift = chunk.bit_length() - 1
    ti = lax.broadcasted_iota(jnp.int32, (chunk, chunk), 0)
    si = lax.broadcasted_iota(jnp.int32, (chunk, chunk), 1)
    tri = (si <= ti).astype(BF16)
    tb = lax.broadcasted_iota(jnp.int32, (span, span), 0)
    sb = lax.broadcasted_iota(jnp.int32, (span, span), 1)
    causal = ((tb >> shift) == (sb >> shift)) & (sb <= tb)

    bs, bls = [], []
    for c in range(rows // chunk):
        hi, mid, lo = _split3(la_ref[c * chunk:(c + 1) * chunk, :])
        b_c = _dot(tri, hi) + _dot(tri, mid) + _dot(tri, lo)
        bs.append(b_c)
        bls.append(jnp.broadcast_to(b_c[chunk - 1:chunk, :], b_c.shape))
    b = jnp.concatenate(bs, axis=0)
    b_last = jnp.concatenate(bls, axis=0)
    k = k_ref[...]
    q_e = (q_ref[...] * (dk ** -0.5) * jnp.exp(b)).astype(BF16)
    k_e = (k * jnp.exp(-b)).astype(BF16)
    k_tail = (k * jnp.exp(b_last - b)).astype(BF16)
    decay = jnp.exp(b_last)
    v = v_ref[...].astype(BF16)
    ng = ng_ref[...]
    for h in range(heads):
        ks = slice(h * dk, (h + 1) * dk)
        vs = slice(h * dv, (h + 1) * dv)
        intra = []
        for u in range(rows // span):
            us = slice(u * span, (u + 1) * span)
            attn = jnp.where(causal, _dot_nt(q_e[us, ks], k_e[us, ks]), 0.0).astype(BF16)
            intra.append(_dot(attn, v[us, vs]))
        st = st_ref[h]
        inter = []
        for c in range(rows // chunk):
            sl = slice(c * chunk, (c + 1) * chunk)
            inter.append(_dot_nt(q_e[sl, ks], st.astype(BF16)))
            st = decay[c * chunk:c * chunk + 1, ks] * st + _dot_tn(v[sl, vs], k_tail[sl, ks])
        st_ref[h] = st
        o = jnp.concatenate(intra, axis=0) + jnp.concatenate(inter, axis=0)
        y_ref[:, vs] = (_rms(o, ng) * _silu(og_ref[:, vs])).astype(BF16)

    @pl.when(r == pl.num_programs(1) - 1)
    def _():
        for h in range(heads):
            s_ref[h] = st_ref[h].T


def gla_prompt_core(proj, log_a, norm_g, *, batch, heads, rows, chunk):
    m = proj.shape[0]
    kd = log_a.shape[1]
    vd = (proj.shape[1] - 2 * kd) // 2
    dk, dv = kd // heads, vd // heads
    seq = m // batch
    rows = min(rows, seq)
    assert chunk & (chunk - 1) == 0 and rows % chunk == 0, "chunk ids come from a shift"
    nr = seq // rows
    rowblk = lambda col0, width: pl.BlockSpec((rows, width), lambda b, r: (b * nr + r, col0 // width))
    return pl.pallas_call(
        functools.partial(_gla_chunk_kernel, chunk=chunk, heads=heads, span=min(rows, GLA_INTRA_SPAN)),
        grid=(batch, nr),
        in_specs=[rowblk(0, kd), rowblk(kd, kd), rowblk(2 * kd, vd), rowblk(2 * kd + vd, vd),
                  pl.BlockSpec((rows, kd), lambda b, r: (b * nr + r, 0)),
                  pl.BlockSpec((1, dv), lambda b, r: (0, 0))],
        out_specs=[pl.BlockSpec((rows, vd), lambda b, r: (b * nr + r, 0)),
                   pl.BlockSpec((None, heads, dk, dv), lambda b, r: (b, 0, 0, 0))],
        out_shape=[jax.ShapeDtypeStruct((m, vd), BF16), jax.ShapeDtypeStruct((batch, heads, dk, dv), F32)],
        scratch_shapes=[pltpu.VMEM((heads, dv, dk), F32)],
        compiler_params=_cparams("parallel", "arbitrary"),
        name="gla_prompt_core",
    )(proj, proj, proj, proj, log_a, norm_g.reshape(1, dv))


def _gla_step_kernel(q_ref, kcol_ref, gcol_ref, v_ref, og_ref, s0_ref, ng_ref, y_ref, s_ref):
    heads, dk, _ = s0_ref.shape
    for h in range(heads):
        s_new = jnp.exp(gcol_ref[h]) * s0_ref[h] + kcol_ref[h] * v_ref[h]
        s_ref[h] = s_new
        q = jnp.broadcast_to(q_ref[h] * (dk ** -0.5), (8, dk)).astype(BF16)
        o = _dot(q, s_new.astype(BF16))[0:1, :]
        y_ref[h] = _rms(o, ng_ref[...]) * _silu(og_ref[h])


def gla_decode_step(q, k, log_a, v, og, state, norm_g):
    bs, heads, dk = q.shape
    dv = v.shape[-1]
    row = lambda w: pl.BlockSpec((None, heads, 1, w), lambda b: (b, 0, 0, 0))
    col = pl.BlockSpec((None, heads, dk, 1), lambda b: (b, 0, 0, 0))
    mat = pl.BlockSpec((None, heads, dk, dv), lambda b: (b, 0, 0, 0))
    y, s = pl.pallas_call(
        _gla_step_kernel,
        grid=(bs,),
        in_specs=[row(dk), col, col, row(dv), row(dv), mat, pl.BlockSpec((1, dv), lambda b: (0, 0))],
        out_specs=[row(dv), mat],
        out_shape=[jax.ShapeDtypeStruct((bs, heads, 1, dv), F32), jax.ShapeDtypeStruct((bs, heads, dk, dv), F32)],
        compiler_params=_cparams("parallel"),
        name="gla_decode_step",
    )(q[:, :, None, :], k[..., None], log_a[..., None], v[:, :, None, :], og[:, :, None, :], state,
      norm_g.reshape(1, dv))
    return y.reshape(bs, heads * dv), s


def _pad_axis(a, axis, mult):
    pad = (-a.shape[axis]) % mult
    if pad == 0:
        return a
    widths = [(0, 0)] * a.ndim
    widths[axis] = (0, pad)
    return jnp.pad(a, widths)


def _forward(x_prompt, x_sample, cache_k, cache_v, state_gla, page_table, norm_sub, final_norm_g,
             w_ffn_gate, w_ffn_up, w_ffn_down, w_qkv_a, lambda_q1, lambda_k1, lambda_q2, lambda_k2,
             subln_a, w_o_a, w_proj_g, w_gk1, w_gk2, b_gk, norm_g, w_o_g, cfg):
    batch, seq, d = x_prompt.shape
    bs = x_sample.shape[0]
    depth = norm_sub.shape[0]
    heads, gheads = cfg["attn_heads"], cfg["gla_heads"]
    n_past = page_table.shape[1] * cache_k.shape[2]
    da = w_qkv_a.shape[2] // 3
    vd_attn = da // heads
    kd = w_gk2.shape[2]
    vd = (w_proj_g.shape[2] - 2 * kd) // 2
    dk, dv = kd // gheads, vd // gheads
    bs_pad = -(-bs // BF16_SUBLANES) * BF16_SUBLANES

    xp = x_prompt.reshape(batch * seq, d)
    xs = _pad_axis(x_sample.reshape(bs, d), 0, BF16_SUBLANES)

    tf = cfg["ffn_tf"]
    wgk1 = _pad_axis(w_gk1.astype(BF16), 2, LANES)
    wgk2 = _pad_axis(w_gk2.astype(BF16), 1, LANES)

    cos_p, sin_p = _rotary_tables(jnp.arange(seq), cfg["rot_dim"])
    cos_s, sin_s = _rotary_tables(jnp.full((bs_pad,), n_past, jnp.int32), cfg["rot_dim"])

    def ffn(xp, xs, i, half):
        gain = norm_sub[i, 2 * half]
        xs, wg, wu, wd = ffn_half_cast(xs, gain, w_ffn_gate, w_ffn_up, w_ffn_down, (i, half), tf=tf)
        last = i == depth - 1 and half == 1
        return ffn_half(xp, gain, wg, wu, wd, (), tm=cfg["ffn_tm"], tf=tf,
                        out_gain=final_norm_g if last else None), xs

    proj_t = dict(tm=cfg["proj_tm"], tn=cfg["proj_tn"])
    qkv_t = dict(tm=cfg["qkv_tm"], tn=cfg["qkv_tn"], rot_dim=cfg["rot_dim"])
    out_t = dict(tm=cfg["out_tm"], tn=cfg["qkv_tn"])

    n_attn = w_qkv_a.shape[0]
    kv_p = [jnp.zeros((n_attn, batch * seq, da), F32) for _ in range(2)]
    kv_s = [jnp.zeros((n_attn, bs_pad, da), F32) for _ in range(2)]
    st_p, st_s = [], []
    for i in range(depth):
        j = i // cfg["n_mixers"]
        xp, xs = ffn(xp, xs, i, 0)
        if i % cfg["n_mixers"] == 0:
            lam_init = 0.8 - 0.6 * math.exp(-0.3 * i)
            lam_params = (lambda_q1[j], lambda_k1[j], lambda_q2[j], lambda_k2[j])
            qb, *rest = qkv_project(xs, norm_sub[i, 1], w_qkv_a, j, kv_s, cos_s, sin_s, **qkv_t)
            kv_s, wqkv = rest[:2], tuple(rest[2:])
            tok = lambda t: t[:bs].reshape(bs, heads, vd_attn)
            o = diff_attention_decode(tok(qb).astype(F32), tok(kv_s[0][j]), tok(kv_s[1][j]), cache_k, cache_v, j,
                                      page_table, lam_params, subln_a[j], heads=heads, lam_init=lam_init,
                                      pages_per_step=cfg["decode_pages"])
            a = _pad_axis(o.reshape(bs, da), 0, BF16_SUBLANES).astype(BF16)
            xs, wo = matmul_residual(a, w_o_a, xs, (j,), **out_t)

            qb, *kv_p = qkv_project(xp, norm_sub[i, 1], wqkv, j, kv_p, cos_p, sin_p, **qkv_t)
            a = diff_attention_prompt(qb, *kv_p, j, lam_params, subln_a[j], batch=batch, heads=heads,
                                      tq=cfg["attn_tq"], tk=cfg["attn_tk"], td=cfg["attn_td"], lam_init=lam_init)
            xp = matmul_residual(a, wo, xp, (), **out_t)
        else:
            gate = functools.partial(gla_gate, tm=cfg["proj_tm"], gate_norm=cfg["gla_gate_norm"])
            proj, wproj = norm_matmul(xs, norm_sub[i, 1], w_proj_g, (j,), **proj_t)
            proj = proj[:bs]
            log_a = gate(xs, norm_sub[i, 1], wgk1[j], wgk2[j], b_gk[j])[:bs]
            y, s_new = gla_decode_step(
                proj[:, :kd].reshape(bs, gheads, dk), proj[:, kd:2 * kd].reshape(bs, gheads, dk),
                log_a.reshape(bs, gheads, dk), proj[:, 2 * kd:2 * kd + vd].reshape(bs, gheads, dv),
                proj[:, 2 * kd + vd:].reshape(bs, gheads, dv), state_gla[j], norm_g[j])
            st_s.append(s_new)
            xs, wo = matmul_residual(_pad_axis(y, 0, BF16_SUBLANES).astype(BF16), w_o_g, xs, (j,), **out_t)

            proj = norm_matmul(xp, norm_sub[i, 1], wproj, (), **proj_t)
            log_a = gate(xp, norm_sub[i, 1], wgk1[j], wgk2[j], b_gk[j])
            y, s_fin = gla_prompt_core(proj, log_a, norm_g[j], batch=batch, heads=gheads,
                                       rows=cfg["gla_rows"], chunk=cfg["gla_chunk"])
            st_p.append(s_fin)
            xp = matmul_residual(y, wo, xp, (), **out_t)
        xp, xs = ffn(xp, xs, i, 1)

    yp = xp.reshape(batch, seq, d)
    ys = final_norm(xs, final_norm_g, tm=cfg["out_tm"])[:bs].reshape(bs, 1, d)
    k_p, v_p = (t.reshape(n_attn, batch, seq, heads, vd_attn) for t in kv_p)
    k_s, v_s = (t[:, :bs].reshape(n_attn, bs, 1, heads, vd_attn) for t in kv_s)
    return yp, ys, k_p, v_p, jnp.stack(st_p), k_s, v_s, jnp.stack(st_s)


def kernel(x_prompt, x_sample, cache_k, cache_v, state_gla, page_table, norm_sub, final_norm, w_ffn_gate, w_ffn_up, w_ffn_down, w_qkv_a, lambda_q1, lambda_k1, lambda_q2, lambda_k2, subln_a, w_o_a, w_proj_g, w_gk1, w_gk2, b_gk, norm_g, w_o_g):
    return _forward(x_prompt, x_sample, cache_k, cache_v, state_gla, page_table, norm_sub, final_norm,
                    w_ffn_gate, w_ffn_up, w_ffn_down, w_qkv_a, lambda_q1, lambda_k1, lambda_q2, lambda_k2,
                    subln_a, w_o_a, w_proj_g, w_gk1, w_gk2, b_gk, norm_g, w_o_g, CFG)
```

```python
import functools
import math

import jax
import jax.numpy as jnp
from jax import lax
from jax.experimental import pallas as pl
from jax.experimental.pallas import tpu as pltpu

F32 = jnp.float32
BF16 = jnp.bfloat16

RMS_EPS = 1e-6
ROPE_THETA = 500000.0
FFN_RESIDUAL_SCALE = 0.5
LOG2_E = 1.4426950408889634
LANES = 128
BF16_SUBLANES = 16
VMEM_LIMIT_BYTES = 56 * 1024 * 1024
GLA_INTRA_SPAN = 256

CFG = dict(
    attn_heads=8, gla_heads=4, rot_dim=32, gla_gate_norm=16.0, gla_chunk=64, n_mixers=2,
    ffn_tm=1024, ffn_tf=512, proj_tm=1024, proj_tn=1024, qkv_tm=1024, qkv_tn=512, out_tm=512,
    attn_tq=2048, attn_tk=512, attn_td=256, decode_pages=8, gla_rows=512,
)


def _cparams(*sem):
    return pltpu.CompilerParams(dimension_semantics=sem, vmem_limit_bytes=VMEM_LIMIT_BYTES)


def _rms(x, gain):
    return x * lax.rsqrt(jnp.mean(x * x, axis=-1, keepdims=True) + RMS_EPS) * gain


def _silu(x):
    return x * jax.nn.sigmoid(x)


def _dot(a, b):
    return jnp.dot(a, b, preferred_element_type=F32)


def _dot_nt(a, b):
    return lax.dot_general(a, b, (((1,), (1,)), ((), ())), preferred_element_type=F32)


def _dot_tn(a, b):
    return lax.dot_general(a, b, (((0,), (0,)), ((), ())), preferred_element_type=F32)


def _layer_spec(block, index_map, lead):
    return pl.BlockSpec((None,) * len(lead) + block, lambda *g: tuple(lead) + tuple(index_map(*g)))


def _ffn_kernel(x_ref, g_ref, wg_ref, wu_ref, wd_ref, *refs):
    o_ref, h_ref = refs[-2:]

    @pl.when(pl.program_id(1) == 0)
    def _():
        x = x_ref[...]
        h_ref[...] = _rms(x, g_ref[...]).astype(BF16)
        o_ref[...] = x

    h = h_ref[...]
    a = (_silu(_dot(h, wg_ref[...])) * (FFN_RESIDUAL_SCALE * _dot(h, wu_ref[...]))).astype(BF16)
    o_ref[...] += _dot(a, wd_ref[...])

    if len(refs) == 3:
        @pl.when(pl.program_id(1) == pl.num_programs(1) - 1)
        def _():
            o_ref[...] = _rms(o_ref[...], refs[0][...])


def ffn_half(x, gain, wg, wu, wd, lead, *, tm, tf, out_gain=None):
    m, d = x.shape
    ffp = wg.shape[-1]
    tm = min(tm, m)
    vec = pl.BlockSpec((1, d), lambda i, f: (0, 0))
    extra = [] if out_gain is None else [out_gain.reshape(1, d)]
    return pl.pallas_call(
        _ffn_kernel,
        grid=(m // tm, ffp // tf),
        in_specs=[
            pl.BlockSpec((tm, d), lambda i, f: (i, 0)),
            vec,
            _layer_spec((d, tf), lambda i, f: (0, f), lead),
            _layer_spec((d, tf), lambda i, f: (0, f), lead),
            _layer_spec((tf, d), lambda i, f: (f, 0), lead),
        ] + [vec] * len(extra),
        out_specs=pl.BlockSpec((tm, d), lambda i, f: (i, 0)),
        out_shape=jax.ShapeDtypeStruct((m, d), F32),
        scratch_shapes=[pltpu.VMEM((tm, d), BF16)],
        compiler_params=_cparams("parallel", "arbitrary"),
        name="ffn_half",
    )(x, gain.reshape(1, d), wg, wu, wd, *extra)


def _ffn_cast_kernel(x_ref, g_ref, wg_ref, wu_ref, wd_ref, o_ref, wgb_ref, wub_ref, wdb_ref, h_ref, *, ff):
    f = pl.program_id(0)
    tf = wg_ref.shape[1]
    valid = ff - f * tf
    col_ok = lax.broadcasted_iota(jnp.int32, (1, tf), 1) < valid
    row_ok = lax.broadcasted_iota(jnp.int32, (tf, 1), 0) < valid
    wg = jnp.where(col_ok, wg_ref[...], 0.0).astype(BF16)
    wu = jnp.where(col_ok, wu_ref[...], 0.0).astype(BF16)
    wd = jnp.where(row_ok, wd_ref[...], 0.0).astype(BF16)
    wgb_ref[...] = wg
    wub_ref[...] = wu
    wdb_ref[...] = wd

    @pl.when(f == 0)
    def _():
        x = x_ref[...]
        h_ref[...] = _rms(x, g_ref[...]).astype(BF16)
        o_ref[...] = x

    h = h_ref[...]
    a = (_silu(_dot(h, wg)) * (FFN_RESIDUAL_SCALE * _dot(h, wu))).astype(BF16)
    o_ref[...] += _dot(a, wd)


def ffn_half_cast(x, gain, wg, wu, wd, lead, *, tf):
    m, d = x.shape
    ff = wg.shape[-1]
    nf = pl.cdiv(ff, tf)
    ffp = nf * tf
    return pl.pallas_call(
        functools.partial(_ffn_cast_kernel, ff=ff),
        grid=(nf,),
        in_specs=[
            pl.BlockSpec((m, d), lambda f: (0, 0)),
            pl.BlockSpec((1, d), lambda f: (0, 0)),
            _layer_spec((d, tf), lambda f: (0, f), lead),
            _layer_spec((d, tf), lambda f: (0, f), lead),
            _layer_spec((tf, d), lambda f: (f, 0), lead),
        ],
        out_specs=[
            pl.BlockSpec((m, d), lambda f: (0, 0)),
            pl.BlockSpec((d, tf), lambda f: (0, f)),
            pl.BlockSpec((d, tf), lambda f: (0, f)),
            pl.BlockSpec((tf, d), lambda f: (f, 0)),
        ],
        out_shape=[jax.ShapeDtypeStruct((m, d), F32), jax.ShapeDtypeStruct((d, ffp), BF16),
                   jax.ShapeDtypeStruct((d, ffp), BF16), jax.ShapeDtypeStruct((ffp, d), BF16)],
        scratch_shapes=[pltpu.VMEM((m, d), BF16)],
        compiler_params=_cparams("arbitrary"),
        name="ffn_half_cast",
    )(x, gain.reshape(1, d), wg, wu, wd)


def _norm_matmul_kernel(x_ref, g_ref, w_ref, o_ref, *refs):
    h_ref = refs[-1]

    @pl.when(pl.program_id(1) == 0)
    def _():
        h_ref[...] = _rms(x_ref[...], g_ref[...]).astype(BF16)

    w = w_ref[...].astype(BF16)
    if len(refs) == 2:
        refs[0][...] = w
    o_ref[...] = _dot(h_ref[...], w)


def norm_matmul(x, gain, w, lead, *, tm, tn):
    m, d = x.shape
    n = w.shape[-1]
    tm = min(tm, m)
    emit = w.dtype != BF16
    assert not emit or m == tm
    ospec = pl.BlockSpec((tm, tn), lambda i, j: (i, j))
    out = pl.pallas_call(
        _norm_matmul_kernel,
        grid=(m // tm, n // tn),
        in_specs=[
            pl.BlockSpec((tm, d), lambda i, j: (i, 0)),
            pl.BlockSpec((1, d), lambda i, j: (0, 0)),
            _layer_spec((d, tn), lambda i, j: (0, j), lead),
        ],
        out_specs=[ospec] + [pl.BlockSpec((d, tn), lambda i, j: (0, j))] * emit,
        out_shape=[jax.ShapeDtypeStruct((m, n), F32)] + [jax.ShapeDtypeStruct((d, n), BF16)] * emit,
        scratch_shapes=[pltpu.VMEM((tm, d), BF16)],
        compiler_params=_cparams("parallel", "arbitrary"),
        name="norm_matmul",
    )(x, gain.reshape(1, d), w)
    return out if emit else out[0]


def _matmul_residual_kernel(a_ref, w_ref, x_ref, o_ref, *wb_ref):
    w = w_ref[...].astype(BF16)
    if wb_ref:
        wb_ref[0][...] = w
    o_ref[...] = x_ref[...] + _dot(a_ref[...], w)


def matmul_residual(a, w, x, lead, *, tm, tn=None):
    m, k = a.shape
    n = w.shape[-1]
    tm = min(tm, m)
    emit = w.dtype != BF16
    tn = tn if emit else n
    assert not emit or m == tm
    xspec = pl.BlockSpec((tm, tn), lambda i, j: (i, j))
    out = pl.pallas_call(
        _matmul_residual_kernel,
        grid=(m // tm, n // tn),
        in_specs=[
            pl.BlockSpec((tm, k), lambda i, j: (i, 0)),
            _layer_spec((k, tn), lambda i, j: (0, j), lead),
            xspec,
        ],
        out_specs=[xspec] + [pl.BlockSpec((k, tn), lambda i, j: (0, j))] * emit,
        out_shape=[jax.ShapeDtypeStruct((m, n), F32)] + [jax.ShapeDtypeStruct((k, n), BF16)] * emit,
        compiler_params=_cparams("parallel", "arbitrary"),
        name="matmul_residual",
    )(a, w, x)
    return out if emit else out[0]


def _final_norm_kernel(x_ref, g_ref, o_ref):
    o_ref[...] = _rms(x_ref[...], g_ref[...])


def final_norm(x, gain, *, tm):
    m, d = x.shape
    tm = min(tm, m)
    return pl.pallas_call(
        _final_norm_kernel,
        grid=(m // tm,),
        in_specs=[pl.BlockSpec((tm, d), lambda i: (i, 0)), pl.BlockSpec((1, d), lambda i: (0, 0))],
        out_specs=pl.BlockSpec((tm, d), lambda i: (i, 0)),
        out_shape=jax.ShapeDtypeStruct((m, d), F32),
        compiler_params=_cparams("parallel"),
        name="final_norm",
    )(x, gain.reshape(1, d))


def _rotary_tables(pos, rot_dim):
    inv_freq = jnp.power(ROPE_THETA, -jnp.arange(0, rot_dim, 2, dtype=F32) / rot_dim)
    ang = pos.astype(F32)[:, None] * inv_freq[None, :]
    cos, sin = jnp.cos(ang), jnp.sin(ang)
    n = pos.shape[0]
    cos_t = jnp.concatenate([cos, cos, jnp.ones((n, LANES - rot_dim), F32)], axis=-1)
    sin_t = jnp.concatenate([-sin, sin, jnp.zeros((n, LANES - rot_dim), F32)], axis=-1)
    return cos_t, sin_t


def _rotate(t, cos_t, sin_t, rot_dim):
    half = rot_dim // 2
    lane = lax.broadcasted_iota(jnp.int32, (t.shape[0], LANES), 1)
    segs = []
    for s in range(t.shape[1] // LANES):
        xs = t[:, s * LANES:(s + 1) * LANES]
        up = pltpu.roll(xs, LANES - half, 1)
        down = pltpu.roll(xs, half, 1)
        partner = jnp.where(lane < half, up, down)
        segs.append(jnp.where(lane < rot_dim, xs * cos_t + partner * sin_t, xs))
    return jnp.concatenate(segs, axis=1)


def _qkv_kernel(x_ref, g_ref, wq_ref, wk_ref, wv_ref, cos_ref, sin_ref, *refs, rot_dim, layer, fresh):
    refs = refs if fresh else refs[2:]
    qb_ref, kf_ref, vf_ref = refs[:3]
    h_ref = refs[-1]
    if fresh:
        kf_ref[...] = jnp.zeros_like(kf_ref)
        vf_ref[...] = jnp.zeros_like(vf_ref)
        kf_ref, vf_ref = kf_ref.at[layer], vf_ref.at[layer]

    @pl.when(pl.program_id(1) == 0)
    def _():
        h_ref[...] = _rms(x_ref[...], g_ref[...]).astype(BF16)

    h = h_ref[...]
    cos_t, sin_t = cos_ref[...], sin_ref[...]
    wq, wk, wv = (w_ref[...].astype(BF16) for w_ref in (wq_ref, wk_ref, wv_ref))
    for w, wb_ref in zip((wq, wk, wv), refs[3:-1]):
        wb_ref[...] = w
    qb_ref[...] = _rotate(_dot(h, wq), cos_t, sin_t, rot_dim).astype(BF16)
    kf_ref[...] = _rotate(_dot(h, wk), cos_t, sin_t, rot_dim)
    vf_ref[...] = _dot(h, wv)


def qkv_project(x, gain, w_qkv, layer, kv_stacks, cos_t, sin_t, *, tm, tn, rot_dim, n_layers):
    m, d = x.shape
    fresh = kv_stacks is None
    tm = min(tm // n_layers if fresh else tm, m)
    emit = not isinstance(w_qkv, tuple)
    if emit:
        assert m == tm
        da = w_qkv.shape[-1] // 3
        nj = da // tn
        ws = (w_qkv,) * 3
        wspecs = [_layer_spec((d, tn), lambda i, j, sec=sec: (0, sec * nj + j), (layer,)) for sec in range(3)]
    else:
        da = w_qkv[0].shape[-1]
        nj = da // tn
        ws = w_qkv
        wspecs = [pl.BlockSpec((d, tn), lambda i, j: (0, j))] * 3
    ntab = cos_t.shape[0] // tm
    tspec = pl.BlockSpec((tm, LANES), lambda i, j: (i % ntab, 0))
    qspec = pl.BlockSpec((tm, tn), lambda i, j: (i, j))
    if fresh:
        kvspec = pl.BlockSpec((n_layers, tm, tn), lambda i, j: (0, i, j))
        stacks, aliases = (), {}
    else:
        kvspec = _layer_spec((tm, tn), lambda i, j: (i, j), (layer,))
        stacks, aliases = tuple(kv_stacks), {7: 1, 8: 2}
    n_w = 3 * emit
    return pl.pallas_call(
        functools.partial(_qkv_kernel, rot_dim=rot_dim, layer=layer, fresh=fresh),
        grid=(m // tm, nj),
        in_specs=[
            pl.BlockSpec((tm, d), lambda i, j: (i, 0)),
            pl.BlockSpec((1, d), lambda i, j: (0, 0)),
            *wspecs, tspec, tspec,
        ] + [pl.BlockSpec(memory_space=pl.ANY)] * len(stacks),
        out_specs=[qspec, kvspec, kvspec] + [pl.BlockSpec((d, tn), lambda i, j: (0, j))] * n_w,
        out_shape=([jax.ShapeDtypeStruct((m, da), BF16)] + [jax.ShapeDtypeStruct((n_layers, m, da), F32)] * 2
                   + [jax.ShapeDtypeStruct((d, da), BF16)] * n_w),
        input_output_aliases=aliases,
        scratch_shapes=[pltpu.VMEM((tm, d), BF16)],
        compiler_params=_cparams("parallel", "arbitrary"),
        name="qkv_project",
    )(x, gain.reshape(1, d), *ws, cos_t, sin_t, *stacks)


def _lambda(lq1_ref, lk1_ref, lq2_ref, lk2_ref, lam_init):
    e1 = jnp.exp(jnp.sum(lq1_ref[...] * lk1_ref[...], axis=-1, keepdims=True))
    e2 = jnp.exp(jnp.sum(lq2_ref[...] * lk2_ref[...], axis=-1, keepdims=True))
    return e1 - e2 + lam_init


def _attn_kernel(q_ref, k_ref, v_ref, lq1_ref, lk1_ref, lq2_ref, lk2_ref, sub_ref,
                 o_ref, kb_ref, vb_ref, m_ref, l_ref, acc_ref, *, tq, tk, td, hd, lam_init):
    i = pl.program_id(2)
    c_scale = (hd ** -0.5) * LOG2_E
    vd = 2 * hd

    @pl.when(i == 0)
    def _():
        kb_ref[...] = k_ref[...].astype(BF16)
        vb_ref[...] = v_ref[...].astype(BF16)

    m_ref[...] = jnp.full_like(m_ref, -jnp.inf)
    l_ref[...] = jnp.zeros_like(l_ref)
    acc_ref[...] = jnp.zeros_like(acc_ref)

    def kv_block(col0, width, row0, masked):
        k = kb_ref[pl.ds(col0, width), :]
        v = vb_ref[pl.ds(col0, width), :]
        q = q_ref[row0:, :]
        if masked:
            visible = (lax.broadcasted_iota(jnp.int32, (tq - row0, width), 1)
                       <= lax.broadcasted_iota(jnp.int32, (tq - row0, width), 0))
        for c in range(2):
            sc = _dot_nt(q[:, c * hd:(c + 1) * hd], k[:, c * hd:(c + 1) * hd]) * c_scale
            if masked:
                sc = jnp.where(visible, sc, -jnp.inf)
            m_old = m_ref[c, row0:, :]
            m_new = jnp.maximum(m_old, jnp.max(sc, axis=-1, keepdims=True))
            alpha = jnp.exp2(m_old - m_new)
            p = jnp.exp2(sc - jnp.tile(m_new, (1, width // LANES)))
            l_ref[c, row0:, :] = alpha * l_ref[c, row0:, :] + jnp.sum(p, axis=-1, keepdims=True)
            acc_ref[c, row0:, :] = (jnp.tile(alpha, (1, vd // LANES)) * acc_ref[c, row0:, :]
                                    + _dot(p.astype(BF16), v))
            m_ref[c, row0:, :] = m_new

    def body(j, carry):
        kv_block(pl.multiple_of(j * tk, tk), tk, 0, False)
        return carry

    lax.fori_loop(0, (i * tq) // tk, body, 0)
    for dj in range(tq // td):
        kv_block(pl.multiple_of(i * tq + dj * td, td), td, dj * td, True)

    lam = _lambda(lq1_ref, lk1_ref, lq2_ref, lk2_ref, lam_init)
    l0 = jnp.tile(l_ref[0], (1, vd // LANES))
    l1 = jnp.tile(l_ref[1], (1, vd // LANES))
    o = acc_ref[0] / l0 - lam * (acc_ref[1] / l1)
    o_ref[...] = (_rms(o, sub_ref[...]) * (1.0 - lam_init)).astype(BF16)


def diff_attention_prompt(q, k, v, layer, lam_params, subln, *, batch, heads, tq, tk, td, lam_init):
    m, da = q.shape
    vd = da // heads
    hd = vd // 2
    seq = m // batch
    tq, tk, td = min(tq, seq), min(tk, seq), min(td, seq)
    assert tq % tk == 0 and tq % td == 0 and seq % tq == 0
    nq = seq // tq
    qspec = pl.BlockSpec((tq, vd), lambda b, h, i: (b * nq + i, h))
    kspec = _layer_spec((seq, vd), lambda b, h, i: (b, h), (layer,))
    pspec = pl.BlockSpec((1, hd), lambda b, h, i: (0, 0))
    return pl.pallas_call(
        functools.partial(_attn_kernel, tq=tq, tk=tk, td=td, hd=hd, lam_init=lam_init),
        grid=(batch, heads, nq),
        in_specs=[qspec, kspec, kspec, pspec, pspec, pspec, pspec, pl.BlockSpec((1, vd), lambda b, h, i: (0, 0))],
        out_specs=qspec,
        out_shape=jax.ShapeDtypeStruct((m, da), BF16),
        scratch_shapes=[pltpu.VMEM((seq, vd), BF16), pltpu.VMEM((seq, vd), BF16),
                        pltpu.VMEM((2, tq, LANES), F32), pltpu.VMEM((2, tq, LANES), F32),
                        pltpu.VMEM((2, tq, vd), F32)],
        compiler_params=_cparams("parallel", "parallel", "arbitrary"),
        name="diff_attention_prompt",
    )(q, k, v, *[p.reshape(1, hd) for p in lam_params], subln.reshape(1, vd))


def _decode_kernel(pt_ref, q_ref, *refs, heads, hd, lam_init, pps):
    kc_refs, vc_refs = refs[:pps], refs[pps:2 * pps]
    (kn_ref, vn_ref, lq1_ref, lk1_ref, lq2_ref, lk2_ref, sub_ref, o_ref, m_ref, l_ref, acc_ref) = refs[2 * pps:]
    p = pl.program_id(1)
    rows = 2 * heads
    c_scale = (hd ** -0.5) * LOG2_E

    @pl.when(p == 0)
    def _():
        m_ref[...] = jnp.full_like(m_ref, -jnp.inf)
        l_ref[...] = jnp.zeros_like(l_ref)
        acc_ref[...] = jnp.zeros_like(acc_ref)

    q = q_ref[...]
    lane = lax.broadcasted_iota(jnp.int32, q.shape, 1)
    qmat = jnp.concatenate([jnp.where(lane < hd, q, 0.0), jnp.where(lane >= hd, q, 0.0)], axis=0).astype(BF16)

    def online_update(scs, pv_of):
        m_old = m_ref[...]
        m_new = m_old
        for sc in scs:
            m_new = jnp.maximum(m_new, jnp.max(sc, axis=-1, keepdims=True))
        alpha = jnp.exp2(m_old - m_new)
        l_new = alpha * l_ref[...]
        acc = alpha * acc_ref[...]
        for r, sc in enumerate(scs):
            pr = jnp.exp2(sc - m_new)
            l_new = l_new + jnp.sum(pr, axis=-1, keepdims=True)
            acc = acc + pv_of(r, pr.astype(BF16))
        l_ref[...] = l_new
        acc_ref[...] = acc
        m_ref[...] = m_new

    n = kc_refs[0].shape[0]
    rh = lax.broadcasted_iota(jnp.int32, (rows, n), 0) & (heads - 1)
    ch = lax.broadcasted_iota(jnp.int32, (rows, n), 1) & (heads - 1)
    same_head = rh == ch
    scs = [jnp.where(same_head, _dot_nt(qmat, kc[...].astype(BF16)) * c_scale, -jnp.inf) for kc in kc_refs]
    online_update(scs, lambda r, pr: _dot(pr, vc_refs[r][...].astype(BF16)))

    @pl.when(p == pl.num_programs(1) - 1)
    def _():
        kn = kn_ref[...].astype(BF16).astype(F32)
        vn = vn_ref[...].astype(BF16).astype(F32)
        kn2 = jnp.concatenate([kn, kn], axis=0)
        vn2 = jnp.concatenate([vn, vn], axis=0)
        sc_new = jnp.sum(qmat.astype(F32) * kn2, axis=-1, keepdims=True) * c_scale
        online_update([sc_new], lambda r, pr: pr.astype(F32) * vn2)
        lam = _lambda(lq1_ref, lk1_ref, lq2_ref, lk2_ref, lam_init)
        on = acc_ref[...] / l_ref[...]
        o = on[:heads] - lam * on[heads:]
        o_ref[...] = _rms(o, sub_ref[...]) * (1.0 - lam_init)


def diff_attention_decode(q, k_new, v_new, cache_k, cache_v, layer, page_table, lam_params, subln,
                          *, heads, lam_init, pages_per_step):
    bs, _, vd = q.shape
    hd = vd // 2
    assert heads & (heads - 1) == 0, "same-head mask uses a power-of-two head count"
    n_layers, n_phys, page, _, _ = cache_k.shape
    n_pages = page_table.shape[1]
    pps = math.gcd(n_pages, pages_per_step)
    ck = cache_k.reshape(n_layers, n_phys, page * heads, vd)
    cv = cache_v.reshape(n_layers, n_phys, page * heads, vd)
    tok = pl.BlockSpec((None, heads, vd), lambda b, p, pt: (b, 0, 0))
    cspec = lambda r: pl.BlockSpec((None, None, page * heads, vd),
                                   lambda b, p, pt: (layer, pt[b * n_pages + p * pps + r], 0, 0))
    pspec = pl.BlockSpec((1, hd), lambda b, p, pt: (0, 0))
    return pl.pallas_call(
        functools.partial(_decode_kernel, heads=heads, hd=hd, lam_init=lam_init, pps=pps),
        grid_spec=pltpu.PrefetchScalarGridSpec(
            num_scalar_prefetch=1,
            grid=(bs, n_pages // pps),
            in_specs=[tok] + [cspec(r) for r in range(pps)] * 2 + [tok, tok, pspec, pspec, pspec, pspec,
                      pl.BlockSpec((1, vd), lambda b, p, pt: (0, 0))],
            out_specs=tok,
            scratch_shapes=[pltpu.VMEM((2 * heads, 1), F32), pltpu.VMEM((2 * heads, 1), F32),
                            pltpu.VMEM((2 * heads, vd), F32)],
        ),
        out_shape=jax.ShapeDtypeStruct((bs, heads, vd), F32),
        compiler_params=_cparams("parallel", "arbitrary"),
        name="diff_attention_decode",
    )(page_table.reshape(-1), q, *([ck] * pps), *([cv] * pps), k_new, v_new,
      *[p.reshape(1, hd) for p in lam_params], subln.reshape(1, vd))


def _log_sigmoid(x):
    return jnp.minimum(x, 0.0) - jnp.log1p(jnp.exp(-jnp.abs(x)))


def _gla_gate_kernel(x_ref, g_ref, w1_ref, w2_ref, b_ref, o_ref, *, gate_norm):
    h = _rms(x_ref[...], g_ref[...]).astype(BF16)
    low = _dot(h, w1_ref[...]).astype(BF16)
    gk = _dot(low, w2_ref[...]) + b_ref[...]
    o_ref[...] = _log_sigmoid(gk) / gate_norm


def gla_gate(x, gain, w1, w2, bias, *, tm, gate_norm):
    m, d = x.shape
    r = w1.shape[1]
    n = w2.shape[1]
    tm = min(tm, m)
    full = lambda shape: pl.BlockSpec(shape, lambda i: (0, 0))
    return pl.pallas_call(
        functools.partial(_gla_gate_kernel, gate_norm=gate_norm),
        grid=(m // tm,),
        in_specs=[pl.BlockSpec((tm, d), lambda i: (i, 0)), full((1, d)), full((d, r)), full((r, n)), full((1, n))],
        out_specs=pl.BlockSpec((tm, n), lambda i: (i, 0)),
        out_shape=jax.ShapeDtypeStruct((m, n), F32),
        compiler_params=_cparams("parallel"),
        name="gla_gate",
    )(x, gain.reshape(1, d), w1, w2, bias.reshape(1, n))


def _split3(x):
    hi = x.astype(BF16)
    r1 = x - hi.astype(F32)
    mid = r1.astype(BF16)
    lo = (r1 - mid.astype(F32)).astype(BF16)
    return hi, mid, lo


def _gla_chunk_kernel(q_ref, k_ref, v_ref, og_ref, la_ref, ng_ref, y_ref, s_ref, st_ref, *, chunk, heads, span):
    r = pl.program_id(1)

    @pl.when(r == 0)
    def _():
        st_ref[...] = jnp.zeros_like(st_ref)

    rows = q_ref.shape[0]
    dk = q_ref.shape[1] // heads
    dv = v_ref.shape[1] // heads
    shift = chunk.bit_length() - 1
    ti = lax.broadcasted_iota(jnp.int32, (chunk, chunk), 0)
    si = lax.broadcasted_iota(jnp.int32, (chunk, chunk), 1)
    tri = (si <= ti).astype(BF16)
    tb = lax.broadcasted_iota(jnp.int32, (span, span), 0)
    sb = lax.broadcasted_iota(jnp.int32, (span, span), 1)
    causal = ((tb >> shift) == (sb >> shift)) & (sb <= tb)

    bs, bls = [], []
    for c in range(rows // chunk):
        hi, mid, lo = _split3(la_ref[c * chunk:(c + 1) * chunk, :])
        b_c = _dot(tri, hi) + _dot(tri, mid) + _dot(tri, lo)
        bs.append(b_c)
        bls.append(jnp.broadcast_to(b_c[chunk - 1:chunk, :], b_c.shape))
    b = jnp.concatenate(bs, axis=0)
    b_last = jnp.concatenate(bls, axis=0)
    k = k_ref[...]
    q_e = (q_ref[...] * (dk ** -0.5) * jnp.exp(b)).astype(BF16)
    k_e = (k * jnp.exp(-b)).astype(BF16)
    k_tail = (k * jnp.exp(b_last - b)).astype(BF16)
    decay = jnp.exp(b_last)
    v = v_ref[...].astype(BF16)
    ng = ng_ref[...]
    for h in range(heads):
        ks = slice(h * dk, (h + 1) * dk)
        vs = slice(h * dv, (h + 1) * dv)
        intra = []
        for u in range(rows // span):
            us = slice(u * span, (u + 1) * span)
            attn = jnp.where(causal, _dot_nt(q_e[us, ks], k_e[us, ks]), 0.0).astype(BF16)
            intra.append(_dot(attn, v[us, vs]))
        st = st_ref[h]
        inter = []
        for c in range(rows // chunk):
            sl = slice(c * chunk, (c + 1) * chunk)
            inter.append(_dot_nt(q_e[sl, ks], st.astype(BF16)))
            st = decay[c * chunk:c * chunk + 1, ks] * st + _dot_tn(v[sl, vs], k_tail[sl, ks])
        st_ref[h] = st
        o = jnp.concatenate(intra, axis=0) + jnp.concatenate(inter, axis=0)
        y_ref[:, vs] = (_rms(o, ng) * _silu(og_ref[:, vs])).astype(BF16)

    @pl.when(r == pl.num_programs(1) - 1)
    def _():
        for h in range(heads):
            s_ref[h] = st_ref[h].T


def gla_prompt_core(proj, log_a, norm_g, *, batch, heads, rows, chunk):
    m = proj.shape[0]
    kd = log_a.shape[1]
    vd = (proj.shape[1] - 2 * kd) // 2
    dk, dv = kd // heads, vd // heads
    seq = m // batch
    rows = min(rows, seq)
    assert chunk & (chunk - 1) == 0 and rows % chunk == 0, "chunk ids come from a shift"
    nr = seq // rows
    rowblk = lambda col0, width: pl.BlockSpec((rows, width), lambda b, r: (b * nr + r, col0 // width))
    return pl.pallas_call(
        functools.partial(_gla_chunk_kernel, chunk=chunk, heads=heads, span=min(rows, GLA_INTRA_SPAN)),
        grid=(batch, nr),
        in_specs=[rowblk(0, kd), rowblk(kd, kd), rowblk(2 * kd, vd), rowblk(2 * kd + vd, vd),
                  pl.BlockSpec((rows, kd), lambda b, r: (b * nr + r, 0)),
                  pl.BlockSpec((1, dv), lambda b, r: (0, 0))],
        out_specs=[pl.BlockSpec((rows, vd), lambda b, r: (b * nr + r, 0)),
                   pl.BlockSpec((None, heads, dk, dv), lambda b, r: (b, 0, 0, 0))],
        out_shape=[jax.ShapeDtypeStruct((m, vd), BF16), jax.ShapeDtypeStruct((batch, heads, dk, dv), F32)],
        scratch_shapes=[pltpu.VMEM((heads, dv, dk), F32)],
        compiler_params=_cparams("parallel", "arbitrary"),
        name="gla_prompt_core",
    )(proj, proj, proj, proj, log_a, norm_g.reshape(1, dv))


def _gla_step_kernel(q_ref, kcol_ref, gcol_ref, v_ref, og_ref, s0_ref, ng_ref, y_ref, s_ref):
    heads, dk, _ = s0_ref.shape
    for h in range(heads):
        s_new = jnp.exp(gcol_ref[h]) * s0_ref[h] + kcol_ref[h] * v_ref[h]
        s_ref[h] = s_new
        q = jnp.broadcast_to(q_ref[h] * (dk ** -0.5), (8, dk)).astype(BF16)
        o = _dot(q, s_new.astype(BF16))[0:1, :]
        y_ref[h] = _rms(o, ng_ref[...]) * _silu(og_ref[h])


def gla_decode_step(q, k, log_a, v, og, state, norm_g):
    bs, heads, dk = q.shape
    dv = v.shape[-1]
    row = lambda w: pl.BlockSpec((None, heads, 1, w), lambda b: (b, 0, 0, 0))
    col = pl.BlockSpec((None, heads, dk, 1), lambda b: (b, 0, 0, 0))
    mat = pl.BlockSpec((None, heads, dk, dv), lambda b: (b, 0, 0, 0))
    y, s = pl.pallas_call(
        _gla_step_kernel,
        grid=(bs,),
        in_specs=[row(dk), col, col, row(dv), row(dv), mat, pl.BlockSpec((1, dv), lambda b: (0, 0))],
        out_specs=[row(dv), mat],
        out_shape=[jax.ShapeDtypeStruct((bs, heads, 1, dv), F32), jax.ShapeDtypeStruct((bs, heads, dk, dv), F32)],
        compiler_params=_cparams("parallel"),
        name="gla_decode_step",
    )(q[:, :, None, :], k[..., None], log_a[..., None], v[:, :, None, :], og[:, :, None, :], state,
      norm_g.reshape(1, dv))
    return y.reshape(bs, heads * dv), s


def _pad_axis(a, axis, mult):
    pad = (-a.shape[axis]) % mult
    if pad == 0:
        return a
    widths = [(0, 0)] * a.ndim
    widths[axis] = (0, pad)
    return jnp.pad(a, widths)


def _forward(x_prompt, x_sample, cache_k, cache_v, state_gla, page_table, norm_sub, final_norm_g,
             w_ffn_gate, w_ffn_up, w_ffn_down, w_qkv_a, lambda_q1, lambda_k1, lambda_q2, lambda_k2,
             subln_a, w_o_a, w_proj_g, w_gk1, w_gk2, b_gk, norm_g, w_o_g, cfg):
    batch, seq, d = x_prompt.shape
    bs = x_sample.shape[0]
    depth = norm_sub.shape[0]
    heads, gheads = cfg["attn_heads"], cfg["gla_heads"]
    n_past = page_table.shape[1] * cache_k.shape[2]
    da = w_qkv_a.shape[2] // 3
    vd_attn = da // heads
    kd = w_gk2.shape[2]
    vd = (w_proj_g.shape[2] - 2 * kd) // 2
    dk, dv = kd // gheads, vd // gheads
    bs_pad = -(-bs // BF16_SUBLANES) * BF16_SUBLANES

    xp = x_prompt.reshape(batch * seq, d)
    xs = _pad_axis(x_sample.reshape(bs, d), 0, BF16_SUBLANES)

    tf = cfg["ffn_tf"]
    wgk1 = _pad_axis(w_gk1.astype(BF16), 2, LANES)
    wgk2 = _pad_axis(w_gk2.astype(BF16), 1, LANES)

    cos_p, sin_p = _rotary_tables(jnp.arange(seq), cfg["rot_dim"])
    cos_s, sin_s = _rotary_tables(jnp.full((bs_pad,), n_past, jnp.int32), cfg["rot_dim"])

    def ffn(xp, xs, i, half):
        gain = norm_sub[i, 2 * half]
        xs, wg, wu, wd = ffn_half_cast(xs, gain, w_ffn_gate, w_ffn_up, w_ffn_down, (i, half), tf=tf)
        last = i == depth - 1 and half == 1
        return ffn_half(xp, gain, wg, wu, wd, (), tm=cfg["ffn_tm"], tf=tf,
                        out_gain=final_norm_g if last else None), xs

    proj_t = dict(tm=cfg["proj_tm"], tn=cfg["proj_tn"])
    qkv_t = dict(tm=cfg["qkv_tm"], tn=cfg["qkv_tn"], rot_dim=cfg["rot_dim"], n_layers=w_qkv_a.shape[0])
    out_t = dict(tm=cfg["out_tm"], tn=cfg["qkv_tn"])

    n_attn = w_qkv_a.shape[0]
    kv_p, kv_s, st_p, st_s = None, None, [], []
    for i in range(depth):
        j = i // cfg["n_mixers"]
        xp, xs = ffn(xp, xs, i, 0)
        if i % cfg["n_mixers"] == 0:
            lam_init = 0.8 - 0.6 * math.exp(-0.3 * i)
            lam_params = (lambda_q1[j], lambda_k1[j], lambda_q2[j], lambda_k2[j])
            qb, *rest = qkv_project(xs, norm_sub[i, 1], w_qkv_a, j, kv_s, cos_s, sin_s, **qkv_t)
            kv_s, wqkv = rest[:2], tuple(rest[2:])
            tok = lambda t: t[:bs].reshape(bs, heads, vd_attn)
            o = diff_attention_decode(tok(qb).astype(F32), tok(kv_s[0][j]), tok(kv_s[1][j]), cache_k, cache_v, j,
                                      page_table, lam_params, subln_a[j], heads=heads, lam_init=lam_init,
                                      pages_per_step=cfg["decode_pages"])
            a = _pad_axis(o.reshape(bs, da), 0, BF16_SUBLANES).astype(BF16)
            xs, wo = matmul_residual(a, w_o_a, xs, (j,), **out_t)

            qb, *kv_p = qkv_project(xp, norm_sub[i, 1], wqkv, j, kv_p, cos_p, sin_p, **qkv_t)
            a = diff_attention_prompt(qb, *kv_p, j, lam_params, subln_a[j], batch=batch, heads=heads,
                                      tq=cfg["attn_tq"], tk=cfg["attn_tk"], td=cfg["attn_td"], lam_init=lam_init)
            xp = matmul_residual(a, wo, xp, (), **out_t)
        else:
            gate = functools.partial(gla_gate, tm=cfg["proj_tm"], gate_norm=cfg["gla_gate_norm"])
            proj, wproj = norm_matmul(xs, norm_sub[i, 1], w_proj_g, (j,), **proj_t)
            proj = proj[:bs]
            log_a = gate(xs, norm_sub[i, 1], wgk1[j], wgk2[j], b_gk[j])[:bs]
            y, s_new = gla_decode_step(
                proj[:, :kd].reshape(bs, gheads, dk), proj[:, kd:2 * kd].reshape(bs, gheads, dk),
                log_a.reshape(bs, gheads, dk), proj[:, 2 * kd:2 * kd + vd].reshape(bs, gheads, dv),
                proj[:, 2 * kd + vd:].reshape(bs, gheads, dv), state_gla[j], norm_g[j])
            st_s.append(s_new)
            xs, wo = matmul_residual(_pad_axis(y, 0, BF16_SUBLANES).astype(BF16), w_o_g, xs, (j,), **out_t)

            proj = norm_matmul(xp, norm_sub[i, 1], wproj, (), **proj_t)
            log_a = gate(xp, norm_sub[i, 1], wgk1[j], wgk2[j], b_gk[j])
            y, s_fin = gla_prompt_core(proj, log_a, norm_g[j], batch=batch, heads=gheads,
                                       rows=cfg["gla_rows"], chunk=cfg["gla_chunk"])
            st_p.append(s_fin)
            xp = matmul_residual(y, wo, xp, (), **out_t)
        xp, xs = ffn(xp, xs, i, 1)

    yp = xp.reshape(batch, seq, d)
    ys = final_norm(xs, final_norm_g, tm=cfg["out_tm"])[:bs].reshape(bs, 1, d)
    k_p, v_p = (t.reshape(n_attn, batch, seq, heads, vd_attn) for t in kv_p)
    k_s, v_s = (t[:, :bs].reshape(n_attn, bs, 1, heads, vd_attn) for t in kv_s)
    return yp, ys, k_p, v_p, jnp.stack(st_p), k_s, v_s, jnp.stack(st_s)


def kernel(x_prompt, x_sample, cache_k, cache_v, state_gla, page_table, norm_sub, final_norm, w_ffn_gate, w_ffn_up, w_ffn_down, w_qkv_a, lambda_q1, lambda_k1, lambda_q2, lambda_k2, subln_a, w_o_a, w_proj_g, w_gk1, w_gk2, b_gk, norm_g, w_o_g):
    return _forward(x_prompt, x_sample, cache_k, cache_v, state_gla, page_table, norm_sub, final_norm,
                    w_ffn_gate, w_ffn_up, w_ffn_down, w_qkv_a, lambda_q1, lambda_k1, lambda_q2, lambda_k2,
                    subln_a, w_o_a, w_proj_g, w_gk1, w_gk2, b_gk, norm_g, w_o_g, CFG)
```

```python
import functools
import math

import jax
import jax.numpy as jnp
from jax import lax
from jax.experimental import pallas as pl
from jax.experimental.pallas import tpu as pltpu

F32 = jnp.float32
BF16 = jnp.bfloat16

RMS_EPS = 1e-6
ROPE_THETA = 500000.0
FFN_RESIDUAL_SCALE = 0.5
LOG2_E = 1.4426950408889634
LANES = 128
BF16_SUBLANES = 16
VMEM_LIMIT_BYTES = 56 * 1024 * 1024
GLA_INTRA_SPAN = 256

CFG = dict(
    attn_heads=8, gla_heads=4, rot_dim=32, gla_gate_norm=16.0, gla_chunk=64, n_mixers=2,
    ffn_tm=1024, ffn_tf=512, proj_tm=1024, proj_tn=1024, qkv_tm=1024, qkv_tn=512, out_tm=512,
    attn_tq=2048, attn_tk=512, attn_td=256, decode_pages=8, gla_rows=256,
)


def _cparams(*sem):
    return pltpu.CompilerParams(dimension_semantics=sem, vmem_limit_bytes=VMEM_LIMIT_BYTES)


def _rms(x, gain):
    return x * lax.rsqrt(jnp.mean(x * x, axis=-1, keepdims=True) + RMS_EPS) * gain


def _silu(x):
    return x * jax.nn.sigmoid(x)


def _dot(a, b):
    return jnp.dot(a, b, preferred_element_type=F32)


def _dot_nt(a, b):
    return lax.dot_general(a, b, (((1,), (1,)), ((), ())), preferred_element_type=F32)


def _dot_tn(a, b):
    return lax.dot_general(a, b, (((0,), (0,)), ((), ())), preferred_element_type=F32)


def _layer_spec(block, index_map, lead):
    return pl.BlockSpec((None,) * len(lead) + block, lambda *g: tuple(lead) + tuple(index_map(*g)))


def _ffn_kernel(x_ref, g_ref, wg_ref, wu_ref, wd_ref, *refs):
    o_ref, h_ref = refs[-2:]

    @pl.when(pl.program_id(1) == 0)
    def _():
        x = x_ref[...]
        h_ref[...] = _rms(x, g_ref[...]).astype(BF16)
        o_ref[...] = x

    h = h_ref[...]
    a = (_silu(_dot(h, wg_ref[...])) * (FFN_RESIDUAL_SCALE * _dot(h, wu_ref[...]))).astype(BF16)
    o_ref[...] += _dot(a, wd_ref[...])

    if len(refs) == 3:
        @pl.when(pl.program_id(1) == pl.num_programs(1) - 1)
        def _():
            o_ref[...] = _rms(o_ref[...], refs[0][...])


def ffn_half(x, gain, wg, wu, wd, lead, *, tm, tf, out_gain=None):
    m, d = x.shape
    ffp = wg.shape[-1]
    tm = min(tm, m)
    vec = pl.BlockSpec((1, d), lambda i, f: (0, 0))
    extra = [] if out_gain is None else [out_gain.reshape(1, d)]
    return pl.pallas_call(
        _ffn_kernel,
        grid=(m // tm, ffp // tf),
        in_specs=[
            pl.BlockSpec((tm, d), lambda i, f: (i, 0)),
            vec,
            _layer_spec((d, tf), lambda i, f: (0, f), lead),
            _layer_spec((d, tf), lambda i, f: (0, f), lead),
            _layer_spec((tf, d), lambda i, f: (f, 0), lead),
        ] + [vec] * len(extra),
        out_specs=pl.BlockSpec((tm, d), lambda i, f: (i, 0)),
        out_shape=jax.ShapeDtypeStruct((m, d), F32),
        scratch_shapes=[pltpu.VMEM((tm, d), BF16)],
        compiler_params=_cparams("parallel", "arbitrary"),
        name="ffn_half",
    )(x, gain.reshape(1, d), wg, wu, wd, *extra)


def _ffn_cast_kernel(x_ref, g_ref, wg_ref, wu_ref, wd_ref, o_ref, wgb_ref, wub_ref, wdb_ref, h_ref, *, ff):
    f = pl.program_id(0)
    tf = wg_ref.shape[1]
    valid = ff - f * tf
    col_ok = lax.broadcasted_iota(jnp.int32, (1, tf), 1) < valid
    row_ok = lax.broadcasted_iota(jnp.int32, (tf, 1), 0) < valid
    wg = jnp.where(col_ok, wg_ref[...], 0.0).astype(BF16)
    wu = jnp.where(col_ok, wu_ref[...], 0.0).astype(BF16)
    wd = jnp.where(row_ok, wd_ref[...], 0.0).astype(BF16)
    wgb_ref[...] = wg
    wub_ref[...] = wu
    wdb_ref[...] = wd

    @pl.when(f == 0)
    def _():
        x = x_ref[...]
        h_ref[...] = _rms(x, g_ref[...]).astype(BF16)
        o_ref[...] = x

    h = h_ref[...]
    a = (_silu(_dot(h, wg)) * (FFN_RESIDUAL_SCALE * _dot(h, wu))).astype(BF16)
    o_ref[...] += _dot(a, wd)


def ffn_half_cast(x, gain, wg, wu, wd, lead, *, tf):
    m, d = x.shape
    ff = wg.shape[-1]
    nf = pl.cdiv(ff, tf)
    ffp = nf * tf
    return pl.pallas_call(
        functools.partial(_ffn_cast_kernel, ff=ff),
        grid=(nf,),
        in_specs=[
            pl.BlockSpec((m, d), lambda f: (0, 0)),
            pl.BlockSpec((1, d), lambda f: (0, 0)),
            _layer_spec((d, tf), lambda f: (0, f), lead),
            _layer_spec((d, tf), lambda f: (0, f), lead),
            _layer_spec((tf, d), lambda f: (f, 0), lead),
        ],
        out_specs=[
            pl.BlockSpec((m, d), lambda f: (0, 0)),
            pl.BlockSpec((d, tf), lambda f: (0, f)),
            pl.BlockSpec((d, tf), lambda f: (0, f)),
            pl.BlockSpec((tf, d), lambda f: (f, 0)),
        ],
        out_shape=[jax.ShapeDtypeStruct((m, d), F32), jax.ShapeDtypeStruct((d, ffp), BF16),
                   jax.ShapeDtypeStruct((d, ffp), BF16), jax.ShapeDtypeStruct((ffp, d), BF16)],
        scratch_shapes=[pltpu.VMEM((m, d), BF16)],
        compiler_params=_cparams("arbitrary"),
        name="ffn_half_cast",
    )(x, gain.reshape(1, d), wg, wu, wd)


def _norm_matmul_kernel(x_ref, g_ref, w_ref, o_ref, *refs):
    h_ref = refs[-1]

    @pl.when(pl.program_id(1) == 0)
    def _():
        h_ref[...] = _rms(x_ref[...], g_ref[...]).astype(BF16)

    w = w_ref[...].astype(BF16)
    if len(refs) == 2:
        refs[0][...] = w
    o_ref[...] = _dot(h_ref[...], w)


def norm_matmul(x, gain, w, lead, *, tm, tn):
    m, d = x.shape
    n = w.shape[-1]
    tm = min(tm, m)
    emit = w.dtype != BF16
    assert not emit or m == tm
    ospec = pl.BlockSpec((tm, tn), lambda i, j: (i, j))
    out = pl.pallas_call(
        _norm_matmul_kernel,
        grid=(m // tm, n // tn),
        in_specs=[
            pl.BlockSpec((tm, d), lambda i, j: (i, 0)),
            pl.BlockSpec((1, d), lambda i, j: (0, 0)),
            _layer_spec((d, tn), lambda i, j: (0, j), lead),
        ],
        out_specs=[ospec] + [pl.BlockSpec((d, tn), lambda i, j: (0, j))] * emit,
        out_shape=[jax.ShapeDtypeStruct((m, n), F32)] + [jax.ShapeDtypeStruct((d, n), BF16)] * emit,
        scratch_shapes=[pltpu.VMEM((tm, d), BF16)],
        compiler_params=_cparams("parallel", "arbitrary"),
        name="norm_matmul",
    )(x, gain.reshape(1, d), w)
    return out if emit else out[0]


def _matmul_residual_kernel(a_ref, w_ref, x_ref, o_ref, *wb_ref):
    w = w_ref[...].astype(BF16)
    if wb_ref:
        wb_ref[0][...] = w
    o_ref[...] = x_ref[...] + _dot(a_ref[...], w)


def matmul_residual(a, w, x, lead, *, tm, tn=None):
    m, k = a.shape
    n = w.shape[-1]
    tm = min(tm, m)
    emit = w.dtype != BF16
    tn = tn if emit else n
    assert not emit or m == tm
    xspec = pl.BlockSpec((tm, tn), lambda i, j: (i, j))
    out = pl.pallas_call(
        _matmul_residual_kernel,
        grid=(m // tm, n // tn),
        in_specs=[
            pl.BlockSpec((tm, k), lambda i, j: (i, 0)),
            _layer_spec((k, tn), lambda i, j: (0, j), lead),
            xspec,
        ],
        out_specs=[xspec] + [pl.BlockSpec((k, tn), lambda i, j: (0, j))] * emit,
        out_shape=[jax.ShapeDtypeStruct((m, n), F32)] + [jax.ShapeDtypeStruct((k, n), BF16)] * emit,
        compiler_params=_cparams("parallel", "arbitrary"),
        name="matmul_residual",
    )(a, w, x)
    return out if emit else out[0]


def _final_norm_kernel(x_ref, g_ref, o_ref):
    o_ref[...] = _rms(x_ref[...], g_ref[...])


def final_norm(x, gain, *, tm):
    m, d = x.shape
    tm = min(tm, m)
    return pl.pallas_call(
        _final_norm_kernel,
        grid=(m // tm,),
        in_specs=[pl.BlockSpec((tm, d), lambda i: (i, 0)), pl.BlockSpec((1, d), lambda i: (0, 0))],
        out_specs=pl.BlockSpec((tm, d), lambda i: (i, 0)),
        out_shape=jax.ShapeDtypeStruct((m, d), F32),
        compiler_params=_cparams("parallel"),
        name="final_norm",
    )(x, gain.reshape(1, d))


def _rotary_tables(pos, rot_dim):
    inv_freq = jnp.power(ROPE_THETA, -jnp.arange(0, rot_dim, 2, dtype=F32) / rot_dim)
    ang = pos.astype(F32)[:, None] * inv_freq[None, :]
    cos, sin = jnp.cos(ang), jnp.sin(ang)
    n = pos.shape[0]
    cos_t = jnp.concatenate([cos, cos, jnp.ones((n, LANES - rot_dim), F32)], axis=-1)
    sin_t = jnp.concatenate([-sin, sin, jnp.zeros((n, LANES - rot_dim), F32)], axis=-1)
    return cos_t, sin_t


def _rotate(t, cos_t, sin_t, rot_dim):
    half = rot_dim // 2
    lane = lax.broadcasted_iota(jnp.int32, (t.shape[0], LANES), 1)
    segs = []
    for s in range(t.shape[1] // LANES):
        xs = t[:, s * LANES:(s + 1) * LANES]
        up = pltpu.roll(xs, LANES - half, 1)
        down = pltpu.roll(xs, half, 1)
        partner = jnp.where(lane < half, up, down)
        segs.append(jnp.where(lane < rot_dim, xs * cos_t + partner * sin_t, xs))
    return jnp.concatenate(segs, axis=1)


def _qkv_kernel(x_ref, g_ref, wq_ref, wk_ref, wv_ref, cos_ref, sin_ref, *refs, rot_dim, layer, fresh):
    refs = refs if fresh else refs[2:]
    qb_ref, kf_ref, vf_ref = refs[:3]
    h_ref = refs[-1]
    if fresh:
        kf_ref[...] = jnp.zeros_like(kf_ref)
        vf_ref[...] = jnp.zeros_like(vf_ref)
        kf_ref, vf_ref = kf_ref.at[layer], vf_ref.at[layer]

    @pl.when(pl.program_id(1) == 0)
    def _():
        h_ref[...] = _rms(x_ref[...], g_ref[...]).astype(BF16)

    h = h_ref[...]
    cos_t, sin_t = cos_ref[...], sin_ref[...]
    wq, wk, wv = (w_ref[...].astype(BF16) for w_ref in (wq_ref, wk_ref, wv_ref))
    for w, wb_ref in zip((wq, wk, wv), refs[3:-1]):
        wb_ref[...] = w
    qb_ref[...] = _rotate(_dot(h, wq), cos_t, sin_t, rot_dim).astype(BF16)
    kf_ref[...] = _rotate(_dot(h, wk), cos_t, sin_t, rot_dim)
    vf_ref[...] = _dot(h, wv)


def qkv_project(x, gain, w_qkv, layer, kv_stacks, cos_t, sin_t, *, tm, tn, rot_dim, n_layers):
    m, d = x.shape
    fresh = kv_stacks is None
    tm = min(tm, m)
    tn = tn // n_layers if fresh else tn
    emit = not isinstance(w_qkv, tuple)
    if emit:
        assert m == tm
        da = w_qkv.shape[-1] // 3
        nj = da // tn
        ws = (w_qkv,) * 3
        wspecs = [_layer_spec((d, tn), lambda i, j, sec=sec: (0, sec * nj + j), (layer,)) for sec in range(3)]
    else:
        da = w_qkv[0].shape[-1]
        nj = da // tn
        ws = w_qkv
        wspecs = [pl.BlockSpec((d, tn), lambda i, j: (0, j))] * 3
    ntab = cos_t.shape[0] // tm
    tspec = pl.BlockSpec((tm, LANES), lambda i, j: (i % ntab, 0))
    qspec = pl.BlockSpec((tm, tn), lambda i, j: (i, j))
    if fresh:
        kvspec = pl.BlockSpec((n_layers, tm, tn), lambda i, j: (0, i, j))
        stacks, aliases = (), {}
    else:
        kvspec = _layer_spec((tm, tn), lambda i, j: (i, j), (layer,))
        stacks, aliases = tuple(kv_stacks), {7: 1, 8: 2}
    n_w = 3 * emit
    return pl.pallas_call(
        functools.partial(_qkv_kernel, rot_dim=rot_dim, layer=layer, fresh=fresh),
        grid=(m // tm, nj),
        in_specs=[
            pl.BlockSpec((tm, d), lambda i, j: (i, 0)),
            pl.BlockSpec((1, d), lambda i, j: (0, 0)),
            *wspecs, tspec, tspec,
        ] + [pl.BlockSpec(memory_space=pl.ANY)] * len(stacks),
        out_specs=[qspec, kvspec, kvspec] + [pl.BlockSpec((d, tn), lambda i, j: (0, j))] * n_w,
        out_shape=([jax.ShapeDtypeStruct((m, da), BF16)] + [jax.ShapeDtypeStruct((n_layers, m, da), F32)] * 2
                   + [jax.ShapeDtypeStruct((d, da), BF16)] * n_w),
        input_output_aliases=aliases,
        scratch_shapes=[pltpu.VMEM((tm, d), BF16)],
        compiler_params=_cparams("parallel", "arbitrary"),
        name="qkv_project",
    )(x, gain.reshape(1, d), *ws, cos_t, sin_t, *stacks)


def _lambda(lq1_ref, lk1_ref, lq2_ref, lk2_ref, lam_init):
    e1 = jnp.exp(jnp.sum(lq1_ref[...] * lk1_ref[...], axis=-1, keepdims=True))
    e2 = jnp.exp(jnp.sum(lq2_ref[...] * lk2_ref[...], axis=-1, keepdims=True))
    return e1 - e2 + lam_init


def _attn_kernel(q_ref, k_ref, v_ref, lq1_ref, lk1_ref, lq2_ref, lk2_ref, sub_ref,
                 o_ref, kb_ref, vb_ref, m_ref, l_ref, acc_ref, *, tq, tk, td, hd, lam_init):
    i = pl.program_id(2)
    c_scale = (hd ** -0.5) * LOG2_E
    vd = 2 * hd

    @pl.when(i == 0)
    def _():
        kb_ref[...] = k_ref[...].astype(BF16)
        vb_ref[...] = v_ref[...].astype(BF16)

    m_ref[...] = jnp.full_like(m_ref, -jnp.inf)
    l_ref[...] = jnp.zeros_like(l_ref)
    acc_ref[...] = jnp.zeros_like(acc_ref)

    def kv_block(col0, width, row0, masked):
        k = kb_ref[pl.ds(col0, width), :]
        v = vb_ref[pl.ds(col0, width), :]
        q = q_ref[row0:, :]
        if masked:
            visible = (lax.broadcasted_iota(jnp.int32, (tq - row0, width), 1)
                       <= lax.broadcasted_iota(jnp.int32, (tq - row0, width), 0))
        for c in range(2):
            sc = _dot_nt(q[:, c * hd:(c + 1) * hd], k[:, c * hd:(c + 1) * hd]) * c_scale
            if masked:
                sc = jnp.where(visible, sc, -jnp.inf)
            m_old = m_ref[c, row0:, :]
            m_new = jnp.maximum(m_old, jnp.max(sc, axis=-1, keepdims=True))
            alpha = jnp.exp2(m_old - m_new)
            p = jnp.exp2(sc - jnp.tile(m_new, (1, width // LANES)))
            l_ref[c, row0:, :] = alpha * l_ref[c, row0:, :] + jnp.sum(p, axis=-1, keepdims=True)
            acc_ref[c, row0:, :] = (jnp.tile(alpha, (1, vd // LANES)) * acc_ref[c, row0:, :]
                                    + _dot(p.astype(BF16), v))
            m_ref[c, row0:, :] = m_new

    def body(j, carry):
        kv_block(pl.multiple_of(j * tk, tk), tk, 0, False)
        return carry

    lax.fori_loop(0, (i * tq) // tk, body, 0)
    for dj in range(tq // td):
        kv_block(pl.multiple_of(i * tq + dj * td, td), td, dj * td, True)

    lam = _lambda(lq1_ref, lk1_ref, lq2_ref, lk2_ref, lam_init)
    l0 = jnp.tile(l_ref[0], (1, vd // LANES))
    l1 = jnp.tile(l_ref[1], (1, vd // LANES))
    o = acc_ref[0] / l0 - lam * (acc_ref[1] / l1)
    o_ref[...] = (_rms(o, sub_ref[...]) * (1.0 - lam_init)).astype(BF16)


def diff_attention_prompt(q, k, v, layer, lam_params, subln, *, batch, heads, tq, tk, td, lam_init):
    m, da = q.shape
    vd = da // heads
    hd = vd // 2
    seq = m // batch
    tq, tk, td = min(tq, seq), min(tk, seq), min(td, seq)
    assert tq % tk == 0 and tq % td == 0 and seq % tq == 0
    nq = seq // tq
    qspec = pl.BlockSpec((tq, vd), lambda b, h, i: (b * nq + i, h))
    kspec = _layer_spec((seq, vd), lambda b, h, i: (b, h), (layer,))
    pspec = pl.BlockSpec((1, hd), lambda b, h, i: (0, 0))
    return pl.pallas_call(
        functools.partial(_attn_kernel, tq=tq, tk=tk, td=td, hd=hd, lam_init=lam_init),
        grid=(batch, heads, nq),
        in_specs=[qspec, kspec, kspec, pspec, pspec, pspec, pspec, pl.BlockSpec((1, vd), lambda b, h, i: (0, 0))],
        out_specs=qspec,
        out_shape=jax.ShapeDtypeStruct((m, da), BF16),
        scratch_shapes=[pltpu.VMEM((seq, vd), BF16), pltpu.VMEM((seq, vd), BF16),
                        pltpu.VMEM((2, tq, LANES), F32), pltpu.VMEM((2, tq, LANES), F32),
                        pltpu.VMEM((2, tq, vd), F32)],
        compiler_params=_cparams("parallel", "parallel", "arbitrary"),
        name="diff_attention_prompt",
    )(q, k, v, *[p.reshape(1, hd) for p in lam_params], subln.reshape(1, vd))


def _decode_kernel(pt_ref, q_ref, *refs, heads, hd, lam_init, pps):
    kc_refs, vc_refs = refs[:pps], refs[pps:2 * pps]
    (kn_ref, vn_ref, lq1_ref, lk1_ref, lq2_ref, lk2_ref, sub_ref, o_ref, m_ref, l_ref, acc_ref) = refs[2 * pps:]
    p = pl.program_id(1)
    rows = 2 * heads
    c_scale = (hd ** -0.5) * LOG2_E

    @pl.when(p == 0)
    def _():
        m_ref[...] = jnp.full_like(m_ref, -jnp.inf)
        l_ref[...] = jnp.zeros_like(l_ref)
        acc_ref[...] = jnp.zeros_like(acc_ref)

    q = q_ref[...]
    lane = lax.broadcasted_iota(jnp.int32, q.shape, 1)
    qmat = jnp.concatenate([jnp.where(lane < hd, q, 0.0), jnp.where(lane >= hd, q, 0.0)], axis=0).astype(BF16)

    def online_update(scs, pv_of):
        m_old = m_ref[...]
        m_new = m_old
        for sc in scs:
            m_new = jnp.maximum(m_new, jnp.max(sc, axis=-1, keepdims=True))
        alpha = jnp.exp2(m_old - m_new)
        l_new = alpha * l_ref[...]
        acc = alpha * acc_ref[...]
        for r, sc in enumerate(scs):
            pr = jnp.exp2(sc - m_new)
            l_new = l_new + jnp.sum(pr, axis=-1, keepdims=True)
            acc = acc + pv_of(r, pr.astype(BF16))
        l_ref[...] = l_new
        acc_ref[...] = acc
        m_ref[...] = m_new

    n = kc_refs[0].shape[0]
    rh = lax.broadcasted_iota(jnp.int32, (rows, n), 0) & (heads - 1)
    ch = lax.broadcasted_iota(jnp.int32, (rows, n), 1) & (heads - 1)
    same_head = rh == ch
    scs = [jnp.where(same_head, _dot_nt(qmat, kc[...].astype(BF16)) * c_scale, -jnp.inf) for kc in kc_refs]
    online_update(scs, lambda r, pr: _dot(pr, vc_refs[r][...].astype(BF16)))

    @pl.when(p == pl.num_programs(1) - 1)
    def _():
        kn = kn_ref[...].astype(BF16).astype(F32)
        vn = vn_ref[...].astype(BF16).astype(F32)
        kn2 = jnp.concatenate([kn, kn], axis=0)
        vn2 = jnp.concatenate([vn, vn], axis=0)
        sc_new = jnp.sum(qmat.astype(F32) * kn2, axis=-1, keepdims=True) * c_scale
        online_update([sc_new], lambda r, pr: pr.astype(F32) * vn2)
        lam = _lambda(lq1_ref, lk1_ref, lq2_ref, lk2_ref, lam_init)
        on = acc_ref[...] / l_ref[...]
        o = on[:heads] - lam * on[heads:]
        o_ref[...] = _rms(o, sub_ref[...]) * (1.0 - lam_init)


def diff_attention_decode(q, k_new, v_new, cache_k, cache_v, layer, page_table, lam_params, subln,
                          *, heads, lam_init, pages_per_step):
    bs, _, vd = q.shape
    hd = vd // 2
    assert heads & (heads - 1) == 0, "same-head mask uses a power-of-two head count"
    n_layers, n_phys, page, _, _ = cache_k.shape
    n_pages = page_table.shape[1]
    pps = math.gcd(n_pages, pages_per_step)
    ck = cache_k.reshape(n_layers, n_phys, page * heads, vd)
    cv = cache_v.reshape(n_layers, n_phys, page * heads, vd)
    tok = pl.BlockSpec((None, heads, vd), lambda b, p, pt: (b, 0, 0))
    cspec = lambda r: pl.BlockSpec((None, None, page * heads, vd),
                                   lambda b, p, pt: (layer, pt[b * n_pages + p * pps + r], 0, 0))
    pspec = pl.BlockSpec((1, hd), lambda b, p, pt: (0, 0))
    return pl.pallas_call(
        functools.partial(_decode_kernel, heads=heads, hd=hd, lam_init=lam_init, pps=pps),
        grid_spec=pltpu.PrefetchScalarGridSpec(
            num_scalar_prefetch=1,
            grid=(bs, n_pages // pps),
            in_specs=[tok] + [cspec(r) for r in range(pps)] * 2 + [tok, tok, pspec, pspec, pspec, pspec,
                      pl.BlockSpec((1, vd), lambda b, p, pt: (0, 0))],
            out_specs=tok,
            scratch_shapes=[pltpu.VMEM((2 * heads, 1), F32), pltpu.VMEM((2 * heads, 1), F32),
                            pltpu.VMEM((2 * heads, vd), F32)],
        ),
        out_shape=jax.ShapeDtypeStruct((bs, heads, vd), F32),
        compiler_params=_cparams("parallel", "arbitrary"),
        name="diff_attention_decode",
    )(page_table.reshape(-1), q, *([ck] * pps), *([cv] * pps), k_new, v_new,
      *[p.reshape(1, hd) for p in lam_params], subln.reshape(1, vd))


def _log_sigmoid(x):
    return jnp.minimum(x, 0.0) - jnp.log1p(jnp.exp(-jnp.abs(x)))


def _gla_gate_kernel(x_ref, g_ref, w1_ref, w2_ref, b_ref, o_ref, *, gate_norm):
    h = _rms(x_ref[...], g_ref[...]).astype(BF16)
    low = _dot(h, w1_ref[...]).astype(BF16)
    gk = _dot(low, w2_ref[...]) + b_ref[...]
    o_ref[...] = _log_sigmoid(gk) / gate_norm


def gla_gate(x, gain, w1, w2, bias, *, tm, gate_norm):
    m, d = x.shape
    r = w1.shape[1]
    n = w2.shape[1]
    tm = min(tm, m)
    full = lambda shape: pl.BlockSpec(shape, lambda i: (0, 0))
    return pl.pallas_call(
        functools.partial(_gla_gate_kernel, gate_norm=gate_norm),
        grid=(m // tm,),
        in_specs=[pl.BlockSpec((tm, d), lambda i: (i, 0)), full((1, d)), full((d, r)), full((r, n)), full((1, n))],
        out_specs=pl.BlockSpec((tm, n), lambda i: (i, 0)),
        out_shape=jax.ShapeDtypeStruct((m, n), F32),
        compiler_params=_cparams("parallel"),
        name="gla_gate",
    )(x, gain.reshape(1, d), w1, w2, bias.reshape(1, n))


def _split3(x):
    hi = x.astype(BF16)
    r1 = x - hi.astype(F32)
    mid = r1.astype(BF16)
    lo = (r1 - mid.astype(F32)).astype(BF16)
    return hi, mid, lo


def _gla_chunk_kernel(q_ref, k_ref, v_ref, og_ref, la_ref, ng_ref, y_ref, s_ref, st_ref, *, chunk, heads, span):
    r = pl.program_id(1)

    @pl.when(r == 0)
    def _():
        st_ref[...] = jnp.zeros_like(st_ref)

    rows = q_ref.shape[0]
    dk = q_ref.shape[1] // heads
    dv = v_ref.shape[1] // heads
    shift = chunk.bit_length() - 1
    ti = lax.broadcasted_iota(jnp.int32, (chunk, chunk), 0)
    si = lax.broadcasted_iota(jnp.int32, (chunk, chunk), 1)
    tri = (si <= ti).astype(BF16)
    tb = lax.broadcasted_iota(jnp.int32, (span, span), 0)
    sb = lax.broadcasted_iota(jnp.int32, (span, span), 1)
    causal = ((tb >> shift) == (sb >> shift)) & (sb <= tb)

    bs, bls = [], []
    for c in range(rows // chunk):
        hi, mid, lo = _split3(la_ref[c * chunk:(c + 1) * chunk, :])
        b_c = _dot(tri, hi) + _dot(tri, mid) + _dot(tri, lo)
        bs.append(b_c)
        bls.append(jnp.broadcast_to(b_c[chunk - 1:chunk, :], b_c.shape))
    b = jnp.concatenate(bs, axis=0)
    b_last = jnp.concatenate(bls, axis=0)
    k = k_ref[...]
    q_e = (q_ref[...] * (dk ** -0.5) * jnp.exp(b)).astype(BF16)
    k_e = (k * jnp.exp(-b)).astype(BF16)
    k_tail = (k * jnp.exp(b_last - b)).astype(BF16)
    decay = jnp.exp(b_last)
    v = v_ref[...].astype(BF16)
    ng = ng_ref[...]
    for h in range(heads):
        ks = slice(h * dk, (h + 1) * dk)
        vs = slice(h * dv, (h + 1) * dv)
        intra = []
        for u in range(rows // span):
            us = slice(u * span, (u + 1) * span)
            attn = jnp.where(causal, _dot_nt(q_e[us, ks], k_e[us, ks]), 0.0).astype(BF16)
            intra.append(_dot(attn, v[us, vs]))
        st = st_ref[h]
        inter = []
        for c in range(rows // chunk):
            sl = slice(c * chunk, (c + 1) * chunk)
            inter.append(_dot_nt(q_e[sl, ks], st.astype(BF16)))
            st = decay[c * chunk:c * chunk + 1, ks] * st + _dot_tn(v[sl, vs], k_tail[sl, ks])
        st_ref[h] = st
        o = jnp.concatenate(intra, axis=0) + jnp.concatenate(inter, axis=0)
        y_ref[:, vs] = (_rms(o, ng) * _silu(og_ref[:, vs])).astype(BF16)

    @pl.when(r == pl.num_programs(1) - 1)
    def _():
        for h in range(heads):
            s_ref[h] = st_ref[h].T


def gla_prompt_core(proj, log_a, norm_g, *, batch, heads, rows, chunk):
    m = proj.shape[0]
    kd = log_a.shape[1]
    vd = (proj.shape[1] - 2 * kd) // 2
    dk, dv = kd // heads, vd // heads
    seq = m // batch
    rows = min(rows, seq)
    assert chunk & (chunk - 1) == 0 and rows % chunk == 0, "chunk ids come from a shift"
    nr = seq // rows
    rowblk = lambda col0, width: pl.BlockSpec((rows, width), lambda b, r: (b * nr + r, col0 // width))
    return pl.pallas_call(
        functools.partial(_gla_chunk_kernel, chunk=chunk, heads=heads, span=min(rows, GLA_INTRA_SPAN)),
        grid=(batch, nr),
        in_specs=[rowblk(0, kd), rowblk(kd, kd), rowblk(2 * kd, vd), rowblk(2 * kd + vd, vd),
                  pl.BlockSpec((rows, kd), lambda b, r: (b * nr + r, 0)),
                  pl.BlockSpec((1, dv), lambda b, r: (0, 0))],
        out_specs=[pl.BlockSpec((rows, vd), lambda b, r: (b * nr + r, 0)),
                   pl.BlockSpec((None, heads, dk, dv), lambda b, r: (b, 0, 0, 0))],
        out_shape=[jax.ShapeDtypeStruct((m, vd), BF16), jax.ShapeDtypeStruct((batch, heads, dk, dv), F32)],
        scratch_shapes=[pltpu.VMEM((heads, dv, dk), F32)],
        compiler_params=_cparams("parallel", "arbitrary"),
        name="gla_prompt_core",
    )(proj, proj, proj, proj, log_a, norm_g.reshape(1, dv))


def _gla_step_kernel(q_ref, kcol_ref, gcol_ref, v_ref, og_ref, s0_ref, ng_ref, y_ref, s_ref):
    heads, dk, _ = s0_ref.shape
    for h in range(heads):
        s_new = jnp.exp(gcol_ref[h]) * s0_ref[h] + kcol_ref[h] * v_ref[h]
        s_ref[h] = s_new
        q = jnp.broadcast_to(q_ref[h] * (dk ** -0.5), (8, dk)).astype(BF16)
        o = _dot(q, s_new.astype(BF16))[0:1, :]
        y_ref[h] = _rms(o, ng_ref[...]) * _silu(og_ref[h])


def gla_decode_step(q, k, log_a, v, og, state, norm_g):
    bs, heads, dk = q.shape
    dv = v.shape[-1]
    row = lambda w: pl.BlockSpec((None, heads, 1, w), lambda b: (b, 0, 0, 0))
    col = pl.BlockSpec((None, heads, dk, 1), lambda b: (b, 0, 0, 0))
    mat = pl.BlockSpec((None, heads, dk, dv), lambda b: (b, 0, 0, 0))
    y, s = pl.pallas_call(
        _gla_step_kernel,
        grid=(bs,),
        in_specs=[row(dk), col, col, row(dv), row(dv), mat, pl.BlockSpec((1, dv), lambda b: (0, 0))],
        out_specs=[row(dv), mat],
        out_shape=[jax.ShapeDtypeStruct((bs, heads, 1, dv), F32), jax.ShapeDtypeStruct((bs, heads, dk, dv), F32)],
        compiler_params=_cparams("parallel"),
        name="gla_decode_step",
    )(q[:, :, None, :], k[..., None], log_a[..., None], v[:, :, None, :], og[:, :, None, :], state,
      norm_g.reshape(1, dv))
    return y.reshape(bs, heads * dv), s


def _pad_axis(a, axis, mult):
    pad = (-a.shape[axis]) % mult
    if pad == 0:
        return a
    widths = [(0, 0)] * a.ndim
    widths[axis] = (0, pad)
    return jnp.pad(a, widths)


def _forward(x_prompt, x_sample, cache_k, cache_v, state_gla, page_table, norm_sub, final_norm_g,
             w_ffn_gate, w_ffn_up, w_ffn_down, w_qkv_a, lambda_q1, lambda_k1, lambda_q2, lambda_k2,
             subln_a, w_o_a, w_proj_g, w_gk1, w_gk2, b_gk, norm_g, w_o_g, cfg):
    batch, seq, d = x_prompt.shape
    bs = x_sample.shape[0]
    depth = norm_sub.shape[0]
    heads, gheads = cfg["attn_heads"], cfg["gla_heads"]
    n_past = page_table.shape[1] * cache_k.shape[2]
    da = w_qkv_a.shape[2] // 3
    vd_attn = da // heads
    kd = w_gk2.shape[2]
    vd = (w_proj_g.shape[2] - 2 * kd) // 2
    dk, dv = kd // gheads, vd // gheads
    bs_pad = -(-bs // BF16_SUBLANES) * BF16_SUBLANES

    xp = x_prompt.reshape(batch * seq, d)
    xs = _pad_axis(x_sample.reshape(bs, d), 0, BF16_SUBLANES)

    tf = cfg["ffn_tf"]
    wgk1 = _pad_axis(w_gk1.astype(BF16), 2, LANES)
    wgk2 = _pad_axis(w_gk2.astype(BF16), 1, LANES)

    cos_p, sin_p = _rotary_tables(jnp.arange(seq), cfg["rot_dim"])
    cos_s, sin_s = _rotary_tables(jnp.full((bs_pad,), n_past, jnp.int32), cfg["rot_dim"])

    def ffn(xp, xs, i, half):
        gain = norm_sub[i, 2 * half]
        xs, wg, wu, wd = ffn_half_cast(xs, gain, w_ffn_gate, w_ffn_up, w_ffn_down, (i, half), tf=tf)
        last = i == depth - 1 and half == 1
        return ffn_half(xp, gain, wg, wu, wd, (), tm=cfg["ffn_tm"], tf=tf,
                        out_gain=final_norm_g if last else None), xs

    proj_t = dict(tm=cfg["proj_tm"], tn=cfg["proj_tn"])
    qkv_t = dict(tm=cfg["qkv_tm"], tn=cfg["qkv_tn"], rot_dim=cfg["rot_dim"], n_layers=w_qkv_a.shape[0])
    out_t = dict(tm=cfg["out_tm"], tn=cfg["qkv_tn"])

    n_attn = w_qkv_a.shape[0]
    kv_p, kv_s, st_p, st_s = None, None, [], []
    for i in range(depth):
        j = i // cfg["n_mixers"]
        xp, xs = ffn(xp, xs, i, 0)
        if i % cfg["n_mixers"] == 0:
            lam_init = 0.8 - 0.6 * math.exp(-0.3 * i)
            lam_params = (lambda_q1[j], lambda_k1[j], lambda_q2[j], lambda_k2[j])
            qb, *rest = qkv_project(xs, norm_sub[i, 1], w_qkv_a, j, kv_s, cos_s, sin_s, **qkv_t)
            kv_s, wqkv = rest[:2], tuple(rest[2:])
            tok = lambda t: t[:bs].reshape(bs, heads, vd_attn)
            o = diff_attention_decode(tok(qb).astype(F32), tok(kv_s[0][j]), tok(kv_s[1][j]), cache_k, cache_v, j,
                                      page_table, lam_params, subln_a[j], heads=heads, lam_init=lam_init,
                                      pages_per_step=cfg["decode_pages"])
            a = _pad_axis(o.reshape(bs, da), 0, BF16_SUBLANES).astype(BF16)
            xs, wo = matmul_residual(a, w_o_a, xs, (j,), **out_t)

            qb, *kv_p = qkv_project(xp, norm_sub[i, 1], wqkv, j, kv_p, cos_p, sin_p, **qkv_t)
            a = diff_attention_prompt(qb, *kv_p, j, lam_params, subln_a[j], batch=batch, heads=heads,
                                      tq=cfg["attn_tq"], tk=cfg["attn_tk"], td=cfg["attn_td"], lam_init=lam_init)
            xp = matmul_residual(a, wo, xp, (), **out_t)
        else:
            gate = functools.partial(gla_gate, tm=cfg["proj_tm"], gate_norm=cfg["gla_gate_norm"])
            proj, wproj = norm_matmul(xs, norm_sub[i, 1], w_proj_g, (j,), **proj_t)
            proj = proj[:bs]
            log_a = gate(xs, norm_sub[i, 1], wgk1[j], wgk2[j], b_gk[j])[:bs]
            y, s_new = gla_decode_step(
                proj[:, :kd].reshape(bs, gheads, dk), proj[:, kd:2 * kd].reshape(bs, gheads, dk),
                log_a.reshape(bs, gheads, dk), proj[:, 2 * kd:2 * kd + vd].reshape(bs, gheads, dv),
                proj[:, 2 * kd + vd:].reshape(bs, gheads, dv), state_gla[j], norm_g[j])
            st_s.append(s_new)
            xs, wo = matmul_residual(_pad_axis(y, 0, BF16_SUBLANES).astype(BF16), w_o_g, xs, (j,), **out_t)

            proj = norm_matmul(xp, norm_sub[i, 1], wproj, (), **proj_t)
            log_a = gate(xp, norm_sub[i, 1], wgk1[j], wgk2[j], b_gk[j])
            y, s_fin = gla_prompt_core(proj, log_a, norm_g[j], batch=batch, heads=gheads,
                                       rows=cfg["gla_rows"], chunk=cfg["gla_chunk"])
            st_p.append(s_fin)
            xp = matmul_residual(y, wo, xp, (), **out_t)
        xp, xs = ffn(xp, xs, i, 1)

    yp = xp.reshape(batch, seq, d)
    ys = final_norm(xs, final_norm_g, tm=cfg["out_tm"])[:bs].reshape(bs, 1, d)
    k_p, v_p = (t.reshape(n_attn, batch, seq, heads, vd_attn) for t in kv_p)
    k_s, v_s = (t[:, :bs].reshape(n_attn, bs, 1, heads, vd_attn) for t in kv_s)
    return yp, ys, k_p, v_p, jnp.stack(st_p), k_s, v_s, jnp.stack(st_s)


def kernel(x_prompt, x_sample, cache_k, cache_v, state_gla, page_table, norm_sub, final_norm, w_ffn_gate, w_ffn_up, w_ffn_down, w_qkv_a, lambda_q1, lambda_k1, lambda_q2, lambda_k2, subln_a, w_o_a, w_proj_g, w_gk1, w_gk2, b_gk, norm_g, w_o_g):
    return _forward(x_prompt, x_sample, cache_k, cache_v, state_gla, page_table, norm_sub, final_norm,
                    w_ffn_gate, w_ffn_up, w_ffn_down, w_qkv_a, lambda_q1, lambda_k1, lambda_q2, lambda_k2,
                    subln_a, w_o_a, w_proj_g, w_gk1, w_gk2, b_gk, norm_g, w_o_g, CFG)
```
